```python
import jax, jax.numpy as jnp
from jax import lax
import numpy as np

D_MODEL = 2048
BATCH = 2
SEQ = 4096
DEPTH = 1
DEC_BATCH = 128
DEC_SEQ = 4
PAST_LEN = 2048
PAGE_SIZE = 128

D_MIX = D_MODEL
D_ATTN = D_MIX // 2
D_GMLP = D_MIX - D_ATTN
HEAD_DIM = 128
N_HEADS = D_ATTN // HEAD_DIM
GROUP_DIM = 128
N_GROUPS = D_GMLP // GROUP_DIM
MOBA_BLOCK = 256
MOBA_TOPK = 3
Q_BLOCK = 32
CHUNK = 128
D_FF = 11 * D_MODEL // 4
CONV_W = 3
N_MOD = 6
EPS = 1e-6
IN_SPLITS = (D_ATTN, 2 * D_ATTN, 3 * D_ATTN, 3 * D_ATTN + D_GMLP)

kernel_name = "hybrid_moba_gmlp_convffn_step"


def rms_norm(x, g):
    xf = x.astype(jnp.float32)
    y = xf * lax.rsqrt(jnp.mean(xf * xf, axis=-1, keepdims=True) + EPS)
    return (y * g.astype(jnp.float32)).astype(x.dtype)


def alibi_slopes():
    return jnp.asarray(2.0 ** (-8.0 * np.arange(1, N_HEADS + 1) / N_HEADS), dtype=jnp.float32)


def moba_select(q, tq, k_mean):
    g = jnp.einsum('thd,nhd->thn', q, k_mean, preferred_element_type=jnp.float32)
    blk = jnp.arange(k_mean.shape[0])
    past = blk[None, None, :] < (tq // MOBA_BLOCK)[:, None, None]
    g = jnp.where(past, g, -jnp.inf)
    top_val, top_idx = lax.top_k(g, MOBA_TOPK)
    return top_idx, top_val > -jnp.inf


def moba_core(q, tq, k_sel, v_sel, sel_blk, sel_ok, k_own, v_own, p_own, slopes):
    scale = HEAD_DIM ** -0.5
    s_sel = jnp.einsum('thd,thkjd->thkj', q, k_sel, preferred_element_type=jnp.float32) * scale
    key_pos = sel_blk[..., None] * MOBA_BLOCK + jnp.arange(MOBA_BLOCK)
    s_sel = s_sel - slopes[None, :, None, None] * (tq[:, None, None, None] - key_pos).astype(jnp.float32)
    s_sel = jnp.where(sel_ok[..., None], s_sel, -jnp.inf)
    s_own = jnp.einsum('thd,lhd->thl', q, k_own, preferred_element_type=jnp.float32) * scale
    s_own = s_own - slopes[None, :, None] * (tq[:, None, None] - p_own[None, None, :]).astype(jnp.float32)
    s_own = jnp.where(p_own[None, None, :] <= tq[:, None, None], s_own, -jnp.inf)
    T, H, K, BL = s_sel.shape
    p = jax.nn.softmax(jnp.concatenate([s_sel.reshape(T, H, K * BL), s_own], axis=-1), axis=-1)
    out = (jnp.einsum('thkj,thkjd->thd', p[..., :K * BL].reshape(T, H, K, BL), v_sel.astype(jnp.float32))
           + jnp.einsum('thl,lhd->thd', p[..., K * BL:], v_own.astype(jnp.float32)))
    return out.astype(q.dtype)


def moba_prompt(q, k, v, slopes):
    B, S = q.shape[:2]
    nb = -(-S // MOBA_BLOCK)
    nbc = max(nb, MOBA_TOPK)
    pad = nbc * MOBA_BLOCK - S
    kp = jnp.pad(k, ((0, 0), (0, pad), (0, 0), (0, 0)))
    vp = jnp.pad(v, ((0, 0), (0, pad), (0, 0), (0, 0)))
    kb = kp.reshape(B, nbc, MOBA_BLOCK, N_HEADS, HEAD_DIM)
    vb = vp.reshape(B, nbc, MOBA_BLOCK, N_HEADS, HEAD_DIM)
    k_mean = jnp.mean(kb.astype(jnp.float32), axis=2).astype(k.dtype)
    kbh = kb.transpose(0, 3, 1, 2, 4)
    vbh = vb.transpose(0, 3, 1, 2, 4)
    hi = jnp.arange(N_HEADS)[None, :, None]

    def one_block(i):
        t0 = i * Q_BLOCK
        qi = lax.dynamic_slice_in_dim(q, t0, Q_BLOCK, axis=1)
        tq = t0 + jnp.arange(Q_BLOCK)
        own = t0 // MOBA_BLOCK
        k_own = lax.dynamic_slice_in_dim(kp, own * MOBA_BLOCK, MOBA_BLOCK, axis=1)
        v_own = lax.dynamic_slice_in_dim(vp, own * MOBA_BLOCK, MOBA_BLOCK, axis=1)
        p_own = own * MOBA_BLOCK + jnp.arange(MOBA_BLOCK)

        def per_seq(qs, kms, kbs, vbs, kos, vos):
            idx, ok = moba_select(qs, tq, kms)
            return moba_core(qs, tq, kbs[hi, idx], vbs[hi, idx], idx, ok, kos, vos, p_own, slopes)

        return jax.vmap(per_seq)(qi, k_mean, kbh, vbh, k_own, v_own)

    out = lax.map(one_block, jnp.arange(S // Q_BLOCK))
    return out.transpose(1, 0, 2, 3, 4).reshape(B, S, N_HEADS, HEAD_DIM)


def moba_sample(q, k, v, cache_k, cache_v, page_table, slopes):
    T = q.shape[1]
    n_pages = PAST_LEN // PAGE_SIZE
    nbp = PAST_LEN // MOBA_BLOCK
    nbc = max(nbp, MOBA_TOPK)
    own_start = nbp * MOBA_BLOCK
    tq = PAST_LEN + jnp.arange(T)
    p_own = jnp.arange(own_start, PAST_LEN + T)
    hi = jnp.arange(N_HEADS)[None, :, None]

    def per_seq(args):
        qs, ks, vs, pt = args
        kpast = cache_k[pt].reshape(n_pages * PAGE_SIZE, N_HEADS, HEAD_DIM)
        vpast = cache_v[pt].reshape(n_pages * PAGE_SIZE, N_HEADS, HEAD_DIM)
        padr = ((0, (nbc - nbp) * MOBA_BLOCK), (0, 0), (0, 0))
        kb = jnp.pad(kpast[:own_start], padr).reshape(nbc, MOBA_BLOCK, N_HEADS, HEAD_DIM)
        vb = jnp.pad(vpast[:own_start], padr).reshape(nbc, MOBA_BLOCK, N_HEADS, HEAD_DIM)
        k_mean = jnp.mean(kb.astype(jnp.float32), axis=1).astype(kb.dtype)
        idx, ok = moba_select(qs, tq, k_mean)
        kbh = kb.transpose(2, 0, 1, 3)
        vbh = vb.transpose(2, 0, 1, 3)
        k_own = jnp.concatenate([kpast[own_start:], ks.astype(kpast.dtype)], axis=0)
        v_own = jnp.concatenate([vpast[own_start:], vs.astype(vpast.dtype)], axis=0)
        return moba_core(qs, tq, kbh[hi, idx], vbh[hi, idx], idx, ok, k_own, v_own, p_own, slopes)

    return lax.map(per_seq, (q, k, v, page_table))


def spatial_gate(u, vg, w_s, b_s):
    L = u.shape[2]
    w = jnp.tril(w_s[:, :L, :L])
    mixed = jnp.einsum('gts,bcsgd->bctgd', w, vg) + b_s[:, :L].T[None, None, :, :, None]
    return u * mixed


def trunk_layer(x, c, attend, conv_prev, w_ada, b_ada, g_norm_mix, w_in, g_q, g_k, g_v, w_s, b_s,
                g_out_attn, g_out_gmlp, w_out, g_norm_ffn, w_up, w_conv, b_conv, w_down):
    B, T, _ = x.shape
    mod = (jax.nn.silu(c) @ w_ada + b_ada)[:, None, :]
    sh_a, sc_a, gt_a, sh_f, sc_f, gt_f = jnp.split(mod, N_MOD, axis=-1)
    h = rms_norm(x, g_norm_mix) * (1 + sc_a) + sh_a
    q, k, v, u, vg = jnp.split(h @ w_in, IN_SPLITS, axis=-1)
    q = rms_norm(q.reshape(B, T, N_HEADS, HEAD_DIM), g_q)
    k = rms_norm(k.reshape(B, T, N_HEADS, HEAD_DIM), g_k)
    v = v.reshape(B, T, N_HEADS, HEAD_DIM)
    o_attn = attend(q, k, v).reshape(B, T, D_ATTN)
    u = jax.nn.gelu(u).reshape(B, T, N_GROUPS, GROUP_DIM)
    vg = rms_norm(jax.nn.gelu(vg).reshape(B, T, N_GROUPS, GROUP_DIM), g_v)
    L = min(T, CHUNK)
    nc = T // L
    o_gmlp = spatial_gate(u.reshape(B, nc, L, N_GROUPS, GROUP_DIM), vg.reshape(B, nc, L, N_GROUPS, GROUP_DIM),
                          w_s, b_s).reshape(B, T, D_GMLP)
    mix = jnp.concatenate([rms_norm(o_attn, g_out_attn), rms_norm(o_gmlp, g_out_gmlp)], axis=-1)
    x = x + gt_a * (mix @ w_out)
    h2 = rms_norm(x, g_norm_ffn) * (1 + sc_f) + sh_f
    a, p = jnp.split(h2 @ w_up, 2, axis=-1)
    ap = jnp.concatenate([conv_prev.astype(a.dtype), a], axis=1)
    ac = b_conv
    for j in range(CONV_W):
        ac = ac + w_conv[j] * ap[:, j:j + T]
    y = x + gt_f * ((jax.nn.silu(ac) * p) @ w_down)
    return y, k, v, vg.reshape(B, T, D_GMLP), ap[:, T:]


def setup_inputs(seed: int = 0) -> dict:
    key = jax.random.key(seed)
    ks = jax.random.split(key, 32)
    n_pages = PAST_LEN // PAGE_SIZE
    n_used = DEC_BATCH * n_pages
    n_phys = n_used + n_used // 4
    Ld = DEPTH

    def nrm(k, shape, scale=1.0):
        return scale * jax.random.normal(k, shape, jnp.float32)

    def gain(k, shape):
        return 1.0 + 0.02 * jax.random.normal(k, shape, jnp.float32)

    page_table = jax.random.permutation(ks[0], n_phys)[:n_used].reshape(DEC_BATCH, n_pages).astype(jnp.int32)
    return {
        "x_prompt": nrm(ks[1], (BATCH, SEQ, D_MODEL)),
        "x_sample": nrm(ks[2], (DEC_BATCH, DEC_SEQ, D_MODEL)),
        "cache_k": nrm(ks[3], (Ld, n_phys, PAGE_SIZE, N_HEADS, HEAD_DIM)),
        "cache_v": nrm(ks[4], (Ld, n_phys, PAGE_SIZE, N_HEADS, HEAD_DIM)),
        "state_conv": nrm(ks[5], (Ld, DEC_BATCH, CONV_W - 1, D_FF)),
        "page_table": page_table,
        "c_prompt": nrm(ks[6], (BATCH, D_MODEL)),
        "c_sample": nrm(ks[7], (DEC_BATCH, D_MODEL)),
        "w_ada": nrm(ks[8], (Ld, D_MODEL, N_MOD * D_MODEL), 0.5 * D_MODEL ** -0.5),
        "b_ada": nrm(ks[9], (Ld, N_MOD * D_MODEL), 0.01),
        "g_norm_mix": gain(ks[10], (Ld, D_MODEL)),
        "w_in": nrm(ks[11], (Ld, D_MODEL, 3 * D_ATTN + 2 * D_GMLP), D_MODEL ** -0.5),
        "g_q": gain(ks[12], (Ld, HEAD_DIM)),
        "g_k": gain(ks[13], (Ld, HEAD_DIM)),
        "g_v": gain(ks[14], (Ld, N_GROUPS, GROUP_DIM)),
        "w_s": nrm(ks[15], (Ld, N_GROUPS, CHUNK, CHUNK), CHUNK ** -0.5),
        "b_s": gain(ks[16], (Ld, N_GROUPS, CHUNK)),
        "g_out_attn": gain(ks[17], (Ld, D_ATTN)),
        "g_out_gmlp": gain(ks[18], (Ld, D_GMLP)),
        "w_out": nrm(ks[19], (Ld, D_MIX, D_MODEL), D_MIX ** -0.5),
        "g_norm_ffn": gain(ks[20], (Ld, D_MODEL)),
        "w_up": nrm(ks[21], (Ld, D_MODEL, 2 * D_FF), D_MODEL ** -0.5),
        "w_conv": nrm(ks[22], (Ld, CONV_W, D_FF), CONV_W ** -0.5),
        "b_conv": nrm(ks[23], (Ld, D_FF), 0.01),
        "w_down": nrm(ks[24], (Ld, D_FF, D_MODEL), D_FF ** -0.5),
    }


def reference(x_prompt, x_sample, cache_k, cache_v, state_conv, page_table, c_prompt, c_sample,
              w_ada, b_ada, g_norm_mix, w_in, g_q, g_k, g_v, w_s, b_s, g_out_attn, g_out_gmlp, w_out,
              g_norm_ffn, w_up, w_conv, b_conv, w_down):
    slopes = alibi_slopes()
    stacked = (w_ada, b_ada, g_norm_mix, w_in, g_q, g_k, g_v, w_s, b_s, g_out_attn, g_out_gmlp, w_out,
               g_norm_ffn, w_up, w_conv, b_conv, w_down)
    B, S = x_prompt.shape[:2]
    h_p, h_s = x_prompt, x_sample
    kp_l, vp_l, ks_l, vs_l, cp_l, cs_l, gs_l = [], [], [], [], [], [], []
    for layer in range(DEPTH):
        lw = [w[layer] for w in stacked]
        attend_p = lambda q, k, v: moba_prompt(q, k, v, slopes)
        attend_s = lambda q, k, v, ck=cache_k[layer], cv=cache_v[layer]: moba_sample(q, k, v, ck, cv, page_table, slopes)
        conv0 = jnp.zeros((B, CONV_W - 1, D_FF), x_prompt.dtype)
        h_p, k_p, v_p, _, conv_p = trunk_layer(h_p, c_prompt, attend_p, conv0, *lw)
        h_s, k_s, v_s, vg_s, conv_s = trunk_layer(h_s, c_sample, attend_s, state_conv[layer], *lw)
        kp_l.append(k_p.reshape(B, S // PAGE_SIZE, PAGE_SIZE, N_HEADS, HEAD_DIM))
        vp_l.append(v_p.reshape(B, S // PAGE_SIZE, PAGE_SIZE, N_HEADS, HEAD_DIM))
        ks_l.append(k_s)
        vs_l.append(v_s)
        cp_l.append(conv_p)
        cs_l.append(conv_s)
        gs_l.append(vg_s)
    return (h_p, h_s, jnp.stack(kp_l), jnp.stack(vp_l), jnp.stack(ks_l), jnp.stack(vs_l),
            jnp.stack(cp_l), jnp.stack(cs_l), jnp.stack(gs_l))
```

```python
import functools

import numpy as np
import jax
import jax.numpy as jnp
from jax import lax
from jax.experimental import pallas as pl
from jax.experimental.pallas import tpu as pltpu

F32 = jnp.float32
BF16 = jnp.bfloat16

HEAD_DIM = 128
GROUP_DIM = 128
MOBA_BLOCK = 256
MOBA_TOPK = 3
CHUNK = 128
CONV_W = 3
N_MOD = 6
EPS = 1e-6
NEG = -1e30
LANES = 128
SUBLANES = 8
OWN_ROWS = 16
VMEM_LIMIT = 56 * 1024 * 1024

_NT = (((1,), (1,)), ((), ()))


def _cparams(sem):
    return pltpu.CompilerParams(dimension_semantics=sem, vmem_limit_bytes=VMEM_LIMIT)


def _rms(x, g):
    ms = jnp.mean(x * x, axis=-1, keepdims=True)
    return (x * lax.rsqrt(ms + EPS)) * g


def _tile_rows(m, rows):
    r = m.shape[0]
    if r == 1 or r == rows:
        return m
    return jnp.concatenate([m] * (rows // r), axis=0)


def _topk_select(g, valid, n_blocks):
    blk = lax.broadcasted_iota(jnp.int32, g.shape, 0)
    gm = jnp.where(valid, g, -jnp.inf)
    rank = jnp.zeros(g.shape, jnp.int32)
    for m in range(n_blocks):
        row = gm[m:m + 1, :]
        beats = (row > gm) | ((row == gm) & (m < blk))
        rank = rank + beats.astype(jnp.int32)
    return valid & (rank < MOBA_TOPK)


def _ada_kernel(c_ref, w_ref, b_ref, o_ref):
    c = c_ref[...]
    s = (c * jax.nn.sigmoid(c)).astype(BF16)
    o_ref[...] = jnp.dot(s, w_ref[...], preferred_element_type=F32) + b_ref[...]


def _ada(c, w_bf, b):
    m, d = c.shape
    n = w_bf.shape[1]
    tn = 1024
    return pl.pallas_call(
        _ada_kernel,
        grid=(n // tn,),
        in_specs=[pl.BlockSpec((m, d), lambda j: (0, 0)),
                  pl.BlockSpec((d, tn), lambda j: (0, j)),
                  pl.BlockSpec((1, tn), lambda j: (0, j))],
        out_specs=pl.BlockSpec((m, tn), lambda j: (0, j)),
        out_shape=jax.ShapeDtypeStruct((m, n), F32),
        compiler_params=_cparams(("arbitrary",)),
        name="ada_mod",
    )(c, w_bf, b.reshape(1, n))


def _inproj_kernel(x_ref, sc_ref, sh_ref, gn_ref, w_ref, gq_ref, gk_ref, gv_ref,
                   q_ref, k_ref, v_ref, u_ref, vg_ref, h_scr, *, n_heads):
    j = pl.program_id(1)
    tm = x_ref.shape[0]

    @pl.when(j == 0)
    def _():
        y = _rms(x_ref[...], gn_ref[...])
        h = y * (1.0 + _tile_rows(sc_ref[...], tm)) + _tile_rows(sh_ref[...], tm)
        h_scr[...] = h.astype(BF16)

    acc = jnp.dot(h_scr[...], w_ref[...], preferred_element_type=F32)

    def per_head_norm(out_ref, g_of_head):
        for hh in range(n_heads):
            sl = slice(hh * HEAD_DIM, (hh + 1) * HEAD_DIM)
            out_ref[:, sl] = _rms(acc[:, sl], g_of_head(hh)).astype(out_ref.dtype)

    @pl.when(j == 0)
    def _():
        per_head_norm(q_ref, lambda hh: gq_ref[...])

    @pl.when(j == 1)
    def _():
        per_head_norm(k_ref, lambda hh: gk_ref[...])

    @pl.when(j == 2)
    def _():
        v_ref[...] = acc.astype(v_ref.dtype)

    @pl.when(j == 3)
    def _():
        u_ref[...] = jax.nn.gelu(acc).astype(u_ref.dtype)

    @pl.when(j == 4)
    def _():
        for hh in range(n_heads):
            sl = slice(hh * GROUP_DIM, (hh + 1) * GROUP_DIM)
            vg_ref[:, sl] = _rms(jax.nn.gelu(acc[:, sl]), gv_ref[:, sl]).astype(vg_ref.dtype)


def _inproj(x, mod, g_norm, w_in_bf, g_q, g_k, g_v, *, tm, tiles_per_batch, q_dtype, vg_dtype):
    rows, d = x.shape
    n_tot = w_in_bf.shape[1]
    tn = n_tot // 5
    n_heads = tn // HEAD_DIM
    r = mod.shape[1]
    modspec = lambda m: pl.BlockSpec((None, r, d), lambda i, j, m=m: (i // tiles_per_batch, 0, m))
    rowspec = pl.BlockSpec((tm, tn), lambda i, j: (i, 0))
    vec = lambda n: pl.BlockSpec((1, n), lambda i, j: (0, 0))
    outs = pl.pallas_call(
        functools.partial(_inproj_kernel, n_heads=n_heads),
        grid=(rows // tm, 5),
        in_specs=[pl.BlockSpec((tm, d), lambda i, j: (i, 0)),
                  modspec(1), modspec(0), vec(d),
                  pl.BlockSpec((d, tn), lambda i, j: (0, j)),
                  vec(HEAD_DIM), vec(HEAD_DIM), vec(tn)],
        out_specs=[rowspec] * 5,
        out_shape=[jax.ShapeDtypeStruct((rows, tn), dt) for dt in (q_dtype, F32, F32, F32, vg_dtype)],
        scratch_shapes=[pltpu.VMEM((tm, d), BF16)],
        compiler_params=_cparams(("arbitrary", "arbitrary")),
        name="in_proj",
    )(x, mod, mod, g_norm.reshape(1, d), w_in_bf, g_q.reshape(1, -1), g_k.reshape(1, -1), g_v.reshape(1, -1))
    return outs


def _moba_prompt_kernel(slopes_ref, q_ref, k_ref, v_ref, o_ref, kb_scr, vb_scr, km_scr, *, n_blocks):
    h = pl.program_id(1)
    qi = pl.program_id(2)
    blk = MOBA_BLOCK
    scale = HEAD_DIM ** -0.5

    @pl.when(qi == 0)
    def _():
        km_scr[...] = jnp.zeros(km_scr.shape, F32)
        for n in range(n_blocks):
            kblk = k_ref[n * blk:(n + 1) * blk, :]
            kb_scr[n * blk:(n + 1) * blk, :] = kblk.astype(BF16)
            vb_scr[n * blk:(n + 1) * blk, :] = v_ref[n * blk:(n + 1) * blk, :].astype(BF16)
            km_scr[n:n + 1, :] = jnp.mean(kblk, axis=0, keepdims=True)

    q = q_ref[...]
    slope = slopes_ref[h]

    g_all = lax.dot_general(km_scr[...].astype(BF16), q, _NT, preferred_element_type=F32)
    g = g_all[:n_blocks]
    valid = lax.broadcasted_iota(jnp.int32, g.shape, 0) < qi
    sel = _topk_select(g, valid, n_blocks)
    bias = jnp.where(sel, 0.0, NEG).astype(F32)
    bias = jnp.concatenate([bias, jnp.full((LANES - n_blocks, blk), NEG, F32)], axis=0)
    bias_t = bias.T

    rowi = lax.broadcasted_iota(jnp.int32, (blk, blk), 0)
    coli = lax.broadcasted_iota(jnp.int32, (blk, blk), 1)
    rel = (rowi - coli).astype(F32)
    lane = lax.broadcasted_iota(jnp.int32, (blk, LANES), 1)

    def scores(j):
        start = pl.multiple_of(j * blk, blk)
        kj = kb_scr[pl.ds(start, blk), :]
        s = lax.dot_general(q, kj, _NT, preferred_element_type=F32) * scale
        dist = rel + ((qi - j) * blk).astype(F32)
        return s - slope * dist, start

    s, start = scores(qi)
    s = jnp.where(coli <= rowi, s, NEG)
    m = jnp.max(s, axis=1, keepdims=True)
    p = jnp.exp(s - m)
    l = jnp.sum(p, axis=1, keepdims=True)
    acc = jnp.dot(p.astype(BF16), vb_scr[pl.ds(start, blk), :], preferred_element_type=F32)

    def body(j, carry):
        m, l, acc = carry
        s, start = scores(j)
        bj = jnp.max(jnp.where(lane == j, bias_t, NEG), axis=1, keepdims=True)
        s = s + bj
        m_new = jnp.maximum(m, jnp.max(s, axis=1, keepdims=True))
        alpha = jnp.exp(m - m_new)
        p = jnp.exp(s - m_new)
        l = alpha * l + jnp.sum(p, axis=1, keepdims=True)
        acc = alpha * acc + jnp.dot(p.astype(BF16), vb_scr[pl.ds(start, blk), :],
                                    preferred_element_type=F32)
        return m_new, l, acc

    m, l, acc = lax.fori_loop(0, qi, body, (m, l, acc))
    o_ref[...] = (acc / l).astype(o_ref.dtype)


def _moba_prompt(q, k, v, slopes):
    b, s, da = k.shape
    n_heads = da // HEAD_DIM
    n_blocks = s // MOBA_BLOCK
    assert s % MOBA_BLOCK == 0 and n_blocks >= MOBA_TOPK and n_blocks <= LANES
    qspec = pl.BlockSpec((None, MOBA_BLOCK, HEAD_DIM), lambda bb, hh, qi: (bb, qi, hh))
    kvspec = pl.BlockSpec((None, s, HEAD_DIM), lambda bb, hh, qi: (bb, 0, hh))
    return pl.pallas_call(
        functools.partial(_moba_prompt_kernel, n_blocks=n_blocks),
        grid=(b, n_heads, n_blocks),
        in_specs=[pl.BlockSpec(memory_space=pltpu.SMEM), qspec, kvspec, kvspec],
        out_specs=qspec,
        out_shape=jax.ShapeDtypeStruct((b, s, da), F32),
        scratch_shapes=[pltpu.VMEM((s, HEAD_DIM), BF16), pltpu.VMEM((s, HEAD_DIM), BF16),
                        pltpu.VMEM((LANES, HEAD_DIM), F32)],
        compiler_params=_cparams(("arbitrary", "arbitrary", "arbitrary")),
        name="moba_prompt",
    )(slopes, q, k, v)


def _moba_sample_kernel(pt_ref, q_ref, kn_ref, vn_ref, slope_ref, k0_ref, k1_ref, v0_ref, v1_ref,
                        o_ref, wt_scr, g_scr, m_scr, l_scr, out_scr, own_scr,
                        *, n_heads, n_q, past_len, n_blocks):
    del pt_ref
    n = pl.program_id(1)
    da = n_heads * HEAD_DIM
    ncol = n_q * n_heads
    scale = HEAD_DIM ** -0.5
    page = k0_ref.shape[0]

    @pl.when(n == 0)
    def _():
        rowh = lax.broadcasted_iota(jnp.int32, (n_heads, da), 0)
        colh = lax.broadcasted_iota(jnp.int32, (n_heads, da), 1) // HEAD_DIM
        diag = rowh == colh
        pieces = [jnp.where(diag, q_ref[t:t + 1, :], 0.0) for t in range(n_q)]
        pieces.append(jnp.zeros((LANES - ncol, da), F32))
        wt_scr[...] = jnp.concatenate(pieces, axis=0).astype(BF16)

    wt = wt_scr[...]
    col = lax.broadcasted_iota(jnp.int32, (1, LANES), 1)
    tq = (past_len + col // n_heads).astype(F32)
    slope = slope_ref[...]

    def local_scores(kpage, pos0, rows):
        s = lax.dot_general(kpage.astype(BF16), wt, _NT, preferred_element_type=F32) * scale
        pos = (pos0 + lax.broadcasted_iota(jnp.int32, (rows, LANES), 0)).astype(F32)
        return s - slope * (tq - pos)

    k0 = k0_ref[...]
    k1 = k1_ref[...]
    s0 = local_scores(k0, n * MOBA_BLOCK, page)
    s1 = local_scores(k1, n * MOBA_BLOCK + page, page)
    m_n = jnp.maximum(jnp.max(s0, axis=0, keepdims=True), jnp.max(s1, axis=0, keepdims=True))
    p0 = jnp.exp(s0 - m_n)
    p1 = jnp.exp(s1 - m_n)
    l_n = jnp.sum(p0, axis=0, keepdims=True) + jnp.sum(p1, axis=0, keepdims=True)
    out_n = (jnp.dot(p0.T[:ncol].astype(BF16), v0_ref[...].astype(BF16), preferred_element_type=F32)
             + jnp.dot(p1.T[:ncol].astype(BF16), v1_ref[...].astype(BF16), preferred_element_type=F32))
    kmean = (jnp.sum(k0, axis=0, keepdims=True) + jnp.sum(k1, axis=0, keepdims=True)) / MOBA_BLOCK
    g_n = lax.dot_general(jnp.broadcast_to(kmean, (OWN_ROWS, da)).astype(BF16), wt, _NT,
                          preferred_element_type=F32)[0:1]
    g_scr[pl.ds(n, 1), :] = g_n
    m_scr[pl.ds(n, 1), :] = m_n
    l_scr[pl.ds(n, 1), :] = l_n
    out_scr[n] = out_n

    @pl.when(n == n_blocks - 1)
    def _():
        g = g_scr[...]
        sel = _topk_select(g, jnp.ones(g.shape, jnp.bool_), n_blocks)
        own_scr[...] = jnp.zeros(own_scr.shape, F32)
        own_scr[0:n_q, :] = kn_ref[...]
        s_own = local_scores(own_scr[...], past_len, OWN_ROWS)
        r_own = lax.broadcasted_iota(jnp.int32, (OWN_ROWS, LANES), 0)
        ok_own = (r_own < n_q) & (r_own <= col // n_heads)
        s_own = jnp.where(ok_own, s_own, NEG)
        m_all = jnp.where(sel, m_scr[...], NEG)
        m_fin = jnp.maximum(jnp.max(m_all, axis=0, keepdims=True), jnp.max(s_own, axis=0, keepdims=True))
        w = jnp.where(sel, jnp.exp(jnp.where(sel, m_scr[...] - m_fin, 0.0)), 0.0)
        p_own = jnp.exp(s_own - m_fin)
        l_fin = jnp.sum(w * l_scr[...], axis=0, keepdims=True) + jnp.sum(p_own, axis=0, keepdims=True)
        inv = 1.0 / l_fin
        a = jnp.concatenate([w * inv, p_own * inv,
                             jnp.zeros((LANES - n_blocks - OWN_ROWS, LANES), F32)], axis=0)
        a_t = a.T[:ncol]
        comb = jnp.zeros((ncol, da), F32)
        for nn in range(n_blocks):
            comb = comb + a_t[:, nn:nn + 1] * out_scr[nn]
        for r in range(n_q):
            comb = comb + a_t[:, n_blocks + r:n_blocks + r + 1] * vn_ref[r:r + 1, :]
        rowh = lax.broadcasted_iota(jnp.int32, (n_heads, da), 0)
        colh = lax.broadcasted_iota(jnp.int32, (n_heads, da), 1) // HEAD_DIM
        diag = rowh == colh
        for t in range(n_q):
            blk_t = comb[t * n_heads:(t + 1) * n_heads, :]
            o_ref[t:t + 1, :] = jnp.sum(jnp.where(diag, blk_t, 0.0), axis=0, keepdims=True).astype(o_ref.dtype)


def _moba_sample(q, k_new, v_new, cache_k, cache_v, page_table, slopes_np, past_len):
    db, n_q, da = q.shape
    n_heads = da // HEAD_DIM
    page = cache_k.shape[1]
    n_pages = page_table.shape[1]
    n_blocks = past_len // MOBA_BLOCK
    ncol = n_q * n_heads
    assert past_len % MOBA_BLOCK == 0 and MOBA_BLOCK == 2 * page and n_pages * page == past_len
    assert n_blocks >= MOBA_TOPK and n_blocks % SUBLANES == 0 and n_heads == SUBLANES
    assert ncol % 16 == 0 and ncol <= LANES and n_q <= OWN_ROWS
    slope_row = np.zeros((1, LANES), np.float32)
    slope_row[0, :ncol] = np.tile(slopes_np, n_q)
    seqspec = pl.BlockSpec((None, n_q, da), lambda s, n, pt: (s, 0, 0))

    def pagespec(which):
        return pl.BlockSpec((None, page, da), lambda s, n, pt: (pt[s * n_pages + 2 * n + which], 0, 0))

    grid_spec = pltpu.PrefetchScalarGridSpec(
        num_scalar_prefetch=1,
        grid=(db, n_blocks),
        in_specs=[seqspec, seqspec, seqspec, pl.BlockSpec((1, LANES), lambda s, n, pt: (0, 0)),
                  pagespec(0), pagespec(1), pagespec(0), pagespec(1)],
        out_specs=seqspec,
        scratch_shapes=[pltpu.VMEM((LANES, da), BF16),
                        pltpu.VMEM((n_blocks, LANES), F32), pltpu.VMEM((n_blocks, LANES), F32),
                        pltpu.VMEM((n_blocks, LANES), F32),
                        pltpu.VMEM((n_blocks, ncol, da), F32),
                        pltpu.VMEM((OWN_ROWS, da), F32)],
    )
    return pl.pallas_call(
        functools.partial(_moba_sample_kernel, n_heads=n_heads, n_q=n_q, past_len=past_len, n_blocks=n_blocks),
        grid_spec=grid_spec,
        out_shape=jax.ShapeDtypeStruct((db, n_q, da), F32),
        compiler_params=_cparams(("arbitrary", "arbitrary")),
        name="moba_sample",
    )(page_table.reshape(-1), q, k_new, v_new, jnp.asarray(slope_row), cache_k, cache_k, cache_v, cache_v)


def _outproj_kernel(oa_ref, u_ref, vg_ref, x_ref, gt_ref, sc_ref, sh_ref, ga_ref, gg_ref, gf_ref,
                    wo_ref, ws_ref, bs_ref, x1_ref, h2_ref, og_scr, *, time_major, n_groups):
    tm = x_ref.shape[0]
    da = oa_ref.shape[1]
    if time_major:
        n_t = ws_ref.shape[0]
        slab = tm // n_t
        for t in range(n_t):
            mixed = jnp.zeros((slab, da), F32) + bs_ref[t:t + 1, :]
            for s in range(t + 1):
                mixed = mixed + ws_ref[t, s:s + 1, :] * vg_ref[s * slab:(s + 1) * slab, :].astype(F32)
            og_scr[t * slab:(t + 1) * slab, :] = u_ref[t * slab:(t + 1) * slab, :] * mixed
    else:
        tri = (lax.broadcasted_iota(jnp.int32, (CHUNK, CHUNK), 0)
               >= lax.broadcasted_iota(jnp.int32, (CHUNK, CHUNK), 1))
        for gi in range(n_groups):
            sl = slice(gi * GROUP_DIM, (gi + 1) * GROUP_DIM)
            w = jnp.where(tri, ws_ref[gi], 0.0).astype(BF16)
            for c in range(tm // CHUNK):
                rs = slice(c * CHUNK, (c + 1) * CHUNK)
                mixed = jnp.dot(w, vg_ref[rs, sl].astype(BF16), preferred_element_type=F32) + bs_ref[:, sl]
                og_scr[rs, sl] = u_ref[rs, sl] * mixed
    a_n = _rms(oa_ref[...], ga_ref[...]).astype(BF16)
    g_n = _rms(og_scr[...], gg_ref[...]).astype(BF16)
    out = (jnp.dot(a_n, wo_ref[0:da, :], preferred_element_type=F32)
           + jnp.dot(g_n, wo_ref[da:, :], preferred_element_type=F32))
    x1 = x_ref[...] + _tile_rows(gt_ref[...], tm) * out
    x1_ref[...] = x1
    h2 = _rms(x1, gf_ref[...]) * (1.0 + _tile_rows(sc_ref[...], tm)) + _tile_rows(sh_ref[...], tm)
    h2_ref[...] = h2.astype(h2_ref.dtype)


def _outproj(oa, u, vg, x, mod, g_oa, g_og, g_ffn, w_out_bf, ws, bs, *, tm, tiles_per_batch, time_major):
    rows, d = x.shape
    da = oa.shape[1]
    r = mod.shape[1]
    modspec = lambda m: pl.BlockSpec((None, r, d), lambda i, m=m: (i // tiles_per_batch, 0, m))
    row = lambda n: pl.BlockSpec((tm, n), lambda i: (i, 0))
    full = lambda a: pl.BlockSpec(a.shape, lambda i, nd=a.ndim: (0,) * nd)
    g_oa, g_og, g_ffn = g_oa.reshape(1, -1), g_og.reshape(1, -1), g_ffn.reshape(1, -1)
    return pl.pallas_call(
        functools.partial(_outproj_kernel, time_major=time_major, n_groups=da // GROUP_DIM),
        grid=(rows // tm,),
        in_specs=[row(da), row(da), row(da), row(d), modspec(2), modspec(4), modspec(3),
                  full(g_oa), full(g_og), full(g_ffn), full(w_out_bf), full(ws), full(bs)],
        out_specs=[row(d), row(d)],
        out_shape=[jax.ShapeDtypeStruct((rows, d), F32), jax.ShapeDtypeStruct((rows, d), BF16)],
        scratch_shapes=[pltpu.VMEM((tm, da), F32)],
        compiler_params=_cparams(("arbitrary",)),
        name="gmlp_out_proj",
    )(oa, u, vg, x, mod, mod, mod, g_oa, g_og, g_ffn, w_out_bf, ws, bs)


def _ffn_kernel(h_ref, x1_ref, gt_ref, wa_ref, wp_ref, wc_ref, bc_ref, wd_ref, prev_ref,
                y_ref, conv_ref, abuf, halo, acc, *, shift, tiles_per_batch):
    i = pl.program_id(0)
    f = pl.program_id(1)
    nf = pl.num_programs(1)
    tm = h_ref.shape[0]
    keep = (CONV_W - 1) * shift
    off = abuf.shape[0] - tm

    h = h_ref[...]
    a = jnp.dot(h, wa_ref[...], preferred_element_type=F32)
    p = jnp.dot(h, wp_ref[...], preferred_element_type=F32)

    @pl.when(i % tiles_per_batch == 0)
    def _():
        abuf[off - keep:off, :] = prev_ref[...]

    @pl.when(i % tiles_per_batch != 0)
    def _():
        abuf[off - keep:off, :] = halo[f]

    abuf[off:off + tm, :] = a
    ac = bc_ref[...] + wc_ref[CONV_W - 1:CONV_W, :] * a
    for jj in range(CONV_W - 1):
        back = (CONV_W - 1 - jj) * shift
        ac = ac + wc_ref[jj:jj + 1, :] * abuf[off - back:off - back + tm, :]
    last = a[tm - keep:, :]
    halo[f] = last
    conv_ref[...] = last
    gate = (ac * jax.nn.sigmoid(ac)) * p
    part = jnp.dot(gate.astype(BF16), wd_ref[...], preferred_element_type=F32)

    @pl.when(f == 0)
    def _():
        acc[...] = part

    @pl.when(f != 0)
    def _():
        acc[...] += part

    @pl.when(f == nf - 1)
    def _():
        y_ref[...] = x1_ref[...] + _tile_rows(gt_ref[...], tm) * acc[...]


def _ffn(h2, x1, mod, w_up_bf, w_conv, b_conv, w_down_bf, conv_prev, *, tm, tf, tiles_per_batch, shift):
    rows, d = x1.shape
    ff = w_down_bf.shape[0]
    nf = ff // tf
    nb, keep, _ = conv_prev.shape
    r = mod.shape[1]
    off = -(-keep // SUBLANES) * SUBLANES
    n_tiles = rows // tm
    y, tail = pl.pallas_call(
        functools.partial(_ffn_kernel, shift=shift, tiles_per_batch=tiles_per_batch),
        grid=(n_tiles, nf),
        in_specs=[pl.BlockSpec((tm, d), lambda i, f: (i, 0)),
                  pl.BlockSpec((tm, d), lambda i, f: (i, 0)),
                  pl.BlockSpec((None, r, d), lambda i, f: (i // tiles_per_batch, 0, 5)),
                  pl.BlockSpec((d, tf), lambda i, f: (0, f)),
                  pl.BlockSpec((d, tf), lambda i, f: (0, f + nf)),
                  pl.BlockSpec((CONV_W, tf), lambda i, f: (0, f)),
                  pl.BlockSpec((1, tf), lambda i, f: (0, f)),
                  pl.BlockSpec((tf, d), lambda i, f: (f, 0)),
                  pl.BlockSpec((None, keep, tf), lambda i, f: (i // tiles_per_batch, 0, f))],
        out_specs=[pl.BlockSpec((tm, d), lambda i, f: (i, 0)),
                   pl.BlockSpec((None, keep, tf), lambda i, f: (i, 0, f))],
        out_shape=[jax.ShapeDtypeStruct((rows, d), F32), jax.ShapeDtypeStruct((n_tiles, keep, ff), F32)],
        scratch_shapes=[pltpu.VMEM((off + tm, tf), F32), pltpu.VMEM((nf, keep, tf), F32),
                        pltpu.VMEM((tm, d), F32)],
        compiler_params=_cparams(("arbitrary", "arbitrary")),
        name="conv_ffn",
    )(h2, x1, mod, w_up_bf, w_up_bf, w_conv, b_conv.reshape(1, ff), w_down_bf, conv_prev)
    return y, tail[tiles_per_batch - 1::tiles_per_batch]


def kernel(x_prompt, x_sample, cache_k, cache_v, state_conv, page_table, c_prompt, c_sample, w_ada, b_ada, g_norm_mix, w_in, g_q, g_k, g_v, w_s, b_s, g_out_attn, g_out_gmlp, w_out, g_norm_ffn, w_up, w_conv, b_conv, w_down):
    depth = w_ada.shape[0]
    b, s, d = x_prompt.shape
    db, n_q, _ = x_sample.shape
    n_phys, page, n_heads, hd = cache_k.shape[1:]
    da = n_heads * hd
    ff = w_down.shape[1]
    n_groups = w_s.shape[1]
    past_len = page_table.shape[1] * page
    assert hd == HEAD_DIM and da == n_groups * GROUP_DIM and s % CHUNK == 0 and n_q <= CHUNK
    slopes_np = (2.0 ** (-8.0 * np.arange(1, n_heads + 1) / n_heads)).astype(np.float32)
    slopes = jnp.asarray(slopes_np)

    n_c = b + db
    pad_c = -(-n_c // SUBLANES) * SUBLANES
    c_all = jnp.concatenate([c_prompt, c_sample, jnp.zeros((pad_c - n_c, d), F32)], axis=0)

    h_p = x_prompt.reshape(b * s, d)
    h_s = x_sample.transpose(1, 0, 2).reshape(n_q * db, d)
    tm_p = 512
    outs = [[] for _ in range(7)]
    for layer in range(depth):
        w_ada_bf = w_ada[layer].astype(BF16)
        w_in_bf = w_in[layer].astype(BF16)
        w_out_bf = w_out[layer].astype(BF16)
        w_up_bf = w_up[layer].astype(BF16)
        w_down_bf = w_down[layer].astype(BF16)

        mod = _ada(c_all, w_ada_bf, b_ada[layer])
        mod_p = mod[:b].reshape(b, 1, N_MOD * d)
        mod_s = mod[b:n_c].reshape(1, db, N_MOD * d)

        q, k, v, u, vg = _inproj(h_p, mod_p, g_norm_mix[layer], w_in_bf, g_q[layer], g_k[layer], g_v[layer],
                                 tm=tm_p, tiles_per_batch=s // tm_p, q_dtype=BF16, vg_dtype=BF16)
        o_attn = _moba_prompt(q.reshape(b, s, da), k.reshape(b, s, da), v.reshape(b, s, da), slopes)
        bs_rows = jnp.repeat(b_s[layer].T, GROUP_DIM, axis=1)
        x1, h2 = _outproj(o_attn.reshape(b * s, da), u, vg, h_p, mod_p, g_out_attn[layer], g_out_gmlp[layer],
                          g_norm_ffn[layer], w_out_bf, w_s[layer], bs_rows,
                          tm=256, tiles_per_batch=s // 256, time_major=False)
        conv0 = jnp.zeros((b, CONV_W - 1, ff), F32)
        y_p, conv_p = _ffn(h2, x1, mod_p, w_up_bf, w_conv[layer], b_conv[layer], w_down_bf, conv0,
                           tm=tm_p, tf=512, tiles_per_batch=s // tm_p, shift=1)

        qs, ks, vs, us, vgs = _inproj(h_s, mod_s, g_norm_mix[layer], w_in_bf, g_q[layer], g_k[layer], g_v[layer],
                                      tm=db, tiles_per_batch=n_q, q_dtype=F32, vg_dtype=F32)
        seq_major = lambda a: a.reshape(n_q, db, -1).transpose(1, 0, 2)
        ks_sm, vs_sm = seq_major(ks), seq_major(vs)
        o_s = _moba_sample(seq_major(qs), ks_sm, vs_sm, cache_k[layer].reshape(n_phys, page, da),
                           cache_v[layer].reshape(n_phys, page, da), page_table, slopes_np, past_len)
        o_s = o_s.transpose(1, 0, 2).reshape(n_q * db, da)
        ws_t = jnp.repeat(jnp.tril(w_s[layer][:, :n_q, :n_q]).transpose(1, 2, 0), GROUP_DIM, axis=2)
        bs_t = jnp.repeat(b_s[layer][:, :n_q].T, GROUP_DIM, axis=1)
        x1s, h2s = _outproj(o_s, us, vgs, h_s, mod_s, g_out_attn[layer], g_out_gmlp[layer], g_norm_ffn[layer],
                            w_out_bf, ws_t, bs_t, tm=n_q * db, tiles_per_batch=1, time_major=True)
        prev_s = state_conv[layer].transpose(1, 0, 2).reshape(1, (CONV_W - 1) * db, ff)
        y_s, conv_s = _ffn(h2s, x1s, mod_s, w_up_bf, w_conv[layer], b_conv[layer], w_down_bf, prev_s,
                           tm=n_q * db, tf=512, tiles_per_batch=1, shift=db)

        h_p, h_s = y_p, y_s
        outs[0].append(k.reshape(b, s // page, page, n_heads, hd))
        outs[1].append(v.reshape(b, s // page, page, n_heads, hd))
        outs[2].append(ks_sm.reshape(db, n_q, n_heads, hd))
        outs[3].append(vs_sm.reshape(db, n_q, n_heads, hd))
        outs[4].append(conv_p)
        outs[5].append(conv_s.reshape(CONV_W - 1, db, ff).transpose(1, 0, 2))
        outs[6].append(seq_major(vgs))

    y_prompt = h_p.reshape(b, s, d)
    y_sample = h_s.reshape(n_q, db, d).transpose(1, 0, 2)
    return (y_prompt, y_sample) + tuple(jnp.stack(o) for o in outs)
```

```python
import functools
import math

import numpy as np
import jax
import jax.numpy as jnp
from jax import lax
from jax.experimental import pallas as pl
from jax.experimental.pallas import tpu as pltpu

F32 = jnp.float32
BF16 = jnp.bfloat16

HEAD_DIM = 128
GROUP_DIM = 128
MOBA_BLOCK = 256
MOBA_TOPK = 3
CHUNK = 128
CONV_W = 3
N_MOD = 6
EPS = 1e-6
NEG = -1e30
LOG2E = math.log2(math.e)
LANES = 128
SUBLANES = 8
VMEM_LIMIT = 56 * 1024 * 1024

_NT = (((1,), (1,)), ((), ()))


def _cparams(sem):
    return pltpu.CompilerParams(dimension_semantics=sem, vmem_limit_bytes=VMEM_LIMIT)


def _rms(x, g):
    ms = jnp.mean(x * x, axis=-1, keepdims=True)
    return (x * lax.rsqrt(ms + EPS)) * g


def _tile_rows(m, rows):
    r = m.shape[0]
    if r == 1 or r == rows:
        return m
    return jnp.concatenate([m] * (rows // r), axis=0)


def _topk_select(g, valid, n_blocks, axis):
    blk = lax.broadcasted_iota(jnp.int32, g.shape, axis)
    gm = jnp.where(valid, g, -jnp.inf)
    rank = jnp.zeros(g.shape, jnp.int32)
    for m in range(n_blocks):
        one = gm[m:m + 1, :] if axis == 0 else gm[:, m:m + 1]
        beats = (one > gm) | ((one == gm) & (m < blk))
        rank = rank + beats.astype(jnp.int32)
    return valid & (rank < MOBA_TOPK)


def _ada_kernel(c_ref, w_ref, b_ref, o_ref):
    c = c_ref[...]
    s = (c * jax.nn.sigmoid(c)).astype(BF16)
    o_ref[...] = jnp.dot(s, w_ref[...], preferred_element_type=F32) + b_ref[...]


def _ada(c, w_bf, b):
    m, d = c.shape
    n = w_bf.shape[1]
    tn = 1024
    return pl.pallas_call(
        _ada_kernel,
        grid=(n // tn,),
        in_specs=[pl.BlockSpec((m, d), lambda j: (0, 0)),
                  pl.BlockSpec((d, tn), lambda j: (0, j)),
                  pl.BlockSpec((1, tn), lambda j: (0, j))],
        out_specs=pl.BlockSpec((m, tn), lambda j: (0, j)),
        out_shape=jax.ShapeDtypeStruct((m, n), F32),
        compiler_params=_cparams(("arbitrary",)),
        name="ada_mod",
    )(c, w_bf, b.reshape(1, n))


def _inproj_kernel(x_ref, sc_ref, sh_ref, gn_ref, w_ref, gq_ref, gk_ref, gv_ref,
                   q_ref, k_ref, v_ref, u_ref, vg_ref, h_scr, *, n_heads):
    j = pl.program_id(1)
    tm = x_ref.shape[0]

    @pl.when(j == 0)
    def _():
        y = _rms(x_ref[...], gn_ref[...])
        h = y * (1.0 + _tile_rows(sc_ref[...], tm)) + _tile_rows(sh_ref[...], tm)
        h_scr[...] = h.astype(BF16)

    acc = jnp.dot(h_scr[...], w_ref[...], preferred_element_type=F32)

    def per_head_norm(out_ref, g_of_head):
        for hh in range(n_heads):
            sl = slice(hh * HEAD_DIM, (hh + 1) * HEAD_DIM)
            out_ref[:, sl] = _rms(acc[:, sl], g_of_head(hh)).astype(out_ref.dtype)

    @pl.when(j == 0)
    def _():
        per_head_norm(q_ref, lambda hh: gq_ref[...])

    @pl.when(j == 1)
    def _():
        per_head_norm(k_ref, lambda hh: gk_ref[...])

    @pl.when(j == 2)
    def _():
        v_ref[...] = acc.astype(v_ref.dtype)

    @pl.when(j == 3)
    def _():
        u_ref[...] = jax.nn.gelu(acc).astype(u_ref.dtype)

    @pl.when(j == 4)
    def _():
        for hh in range(n_heads):
            sl = slice(hh * GROUP_DIM, (hh + 1) * GROUP_DIM)
            vg_ref[:, sl] = _rms(jax.nn.gelu(acc[:, sl]), gv_ref[:, sl]).astype(vg_ref.dtype)


def _inproj(x, mod, g_norm, w_in_bf, g_q, g_k, g_v, *, tm, tiles_per_batch, q_dtype, vg_dtype):
    rows, d = x.shape
    n_tot = w_in_bf.shape[1]
    tn = n_tot // 5
    n_heads = tn // HEAD_DIM
    r = mod.shape[1]
    modspec = lambda m: pl.BlockSpec((None, r, d), lambda i, j, m=m: (i // tiles_per_batch, 0, m))
    rowspec = pl.BlockSpec((tm, tn), lambda i, j: (i, 0))
    vec = lambda n: pl.BlockSpec((1, n), lambda i, j: (0, 0))
    outs = pl.pallas_call(
        functools.partial(_inproj_kernel, n_heads=n_heads),
        grid=(rows // tm, 5),
        in_specs=[pl.BlockSpec((tm, d), lambda i, j: (i, 0)),
                  modspec(1), modspec(0), vec(d),
                  pl.BlockSpec((d, tn), lambda i, j: (0, j)),
                  vec(HEAD_DIM), vec(HEAD_DIM), vec(tn)],
        out_specs=[rowspec] * 5,
        out_shape=[jax.ShapeDtypeStruct((rows, tn), dt) for dt in (q_dtype, F32, F32, F32, vg_dtype)],
        scratch_shapes=[pltpu.VMEM((tm, d), BF16)],
        compiler_params=_cparams(("arbitrary", "arbitrary")),
        name="in_proj",
    )(x, mod, mod, g_norm.reshape(1, d), w_in_bf, g_q.reshape(1, -1), g_k.reshape(1, -1), g_v.reshape(1, -1))
    return outs


def _moba_prompt_kernel(slopes_ref, q_ref, k_ref, v_ref, o_ref, kb_scr, vb_scr, km_scr, al_scr, *, n_blocks):
    h = pl.program_id(1)
    qi = pl.program_id(2)
    blk = MOBA_BLOCK
    tile = 2 * blk
    c1 = HEAD_DIM ** -0.5 * LOG2E
    nslope = -slopes_ref[h] * LOG2E

    @pl.when(qi == 0)
    def _():
        km_scr[...] = jnp.zeros(km_scr.shape, F32)
        lane = lax.broadcasted_iota(jnp.int32, (blk, LANES), 1)
        for n in range(n_blocks):
            rows = slice(n * blk, (n + 1) * blk)
            kblk = k_ref[rows, :]
            kb_scr[rows, 0:HEAD_DIM] = kblk.astype(BF16)
            kb_scr[rows, HEAD_DIM:] = jnp.where(lane == n, 1.0, 0.0).astype(BF16)
            vb_scr[rows, :] = v_ref[rows, :].astype(BF16)
            km_scr[n:n + 1, :] = jnp.mean(kblk, axis=0, keepdims=True)
        rel = (lax.broadcasted_iota(jnp.int32, (blk, tile), 0)
               - lax.broadcasted_iota(jnp.int32, (blk, tile), 1)).astype(F32)
        al_scr[...] = nslope * rel

    q = q_ref[...]

    g_all = lax.dot_general(km_scr[...].astype(BF16), q, _NT, preferred_element_type=F32)
    g = g_all[:n_blocks]
    bidx = lax.broadcasted_iota(jnp.int32, g.shape, 0)
    sel = _topk_select(g, bidx < qi, n_blocks, axis=0)
    bias = jnp.where(sel | (bidx == qi), 0.0, NEG).astype(F32)
    bias = jnp.concatenate([bias, jnp.full((LANES - n_blocks, blk), NEG, F32)], axis=0)
    q_aug = jnp.concatenate([q, bias.T.astype(BF16)], axis=1)

    def tile_scores(t):
        start = pl.multiple_of(t * tile, tile)
        raw = lax.dot_general(q_aug, kb_scr[pl.ds(start, tile), :], _NT, preferred_element_type=F32)
        off = nslope * ((qi - 2 * t) * blk).astype(F32)
        return raw * c1 + al_scr[...], off, start

    t_own = qi // 2
    s, off, start = tile_scores(t_own)
    rowi = lax.broadcasted_iota(jnp.int32, (blk, tile), 0)
    coli = lax.broadcasted_iota(jnp.int32, (blk, tile), 1)
    s = jnp.where(coli <= rowi + (qi - 2 * t_own) * blk, s, NEG)
    m = jnp.max(s, axis=1, keepdims=True) + off
    p = jnp.exp2(s - (m - off))
    l = jnp.sum(p, axis=1, keepdims=True)
    acc = jnp.dot(p.astype(BF16), vb_scr[pl.ds(start, tile), :], preferred_element_type=F32)

    def body(t, carry):
        m, l, acc = carry
        s, off, start = tile_scores(t)
        m_new = jnp.maximum(m, jnp.max(s, axis=1, keepdims=True) + off)
        alpha = jnp.exp2(m - m_new)
        p = jnp.exp2(s - (m_new - off))
        l = alpha * l + jnp.sum(p, axis=1, keepdims=True)
        acc = alpha * acc + jnp.dot(p.astype(BF16), vb_scr[pl.ds(start, tile), :],
                                    preferred_element_type=F32)
        return m_new, l, acc

    m, l, acc = lax.fori_loop(0, t_own, body, (m, l, acc))
    o_ref[...] = (acc / l).astype(o_ref.dtype)


def _moba_prompt(q, k, v, slopes):
    b, s, da = k.shape
    n_heads = da // HEAD_DIM
    n_blocks = s // MOBA_BLOCK
    assert s % (2 * MOBA_BLOCK) == 0 and n_blocks >= MOBA_TOPK and n_blocks <= LANES
    qspec = pl.BlockSpec((None, MOBA_BLOCK, HEAD_DIM), lambda bb, hh, qi: (bb, qi, hh))
    kvspec = pl.BlockSpec((None, s, HEAD_DIM), lambda bb, hh, qi: (bb, 0, hh))
    return pl.pallas_call(
        functools.partial(_moba_prompt_kernel, n_blocks=n_blocks),
        grid=(b, n_heads, n_blocks),
        in_specs=[pl.BlockSpec(memory_space=pltpu.SMEM), qspec, kvspec, kvspec],
        out_specs=qspec,
        out_shape=jax.ShapeDtypeStruct((b, s, da), F32),
        scratch_shapes=[pltpu.VMEM((s, 2 * HEAD_DIM), BF16), pltpu.VMEM((s, HEAD_DIM), BF16),
                        pltpu.VMEM((LANES, HEAD_DIM), F32), pltpu.VMEM((MOBA_BLOCK, 2 * MOBA_BLOCK), F32)],
        compiler_params=_cparams(("arbitrary", "arbitrary", "arbitrary")),
        name="moba_prompt",
    )(slopes, q, k, v)


def _moba_sample_kernel(pt_ref, q_ref, kn_ref, vn_ref, slope_ref, k0_ref, k1_ref, v0_ref, v1_ref,
                        o_ref, q_scr, g_scr, m_scr, l_scr, out_scr, ownk_scr, ownv_scr,
                        *, n_heads, n_q, past_len, n_blocks):
    del pt_ref
    n = pl.program_id(1)
    nrow = n_q * n_heads
    scale = HEAD_DIM ** -0.5
    page = k0_ref.shape[0]
    pkeys = page * n_heads
    lane128 = lax.broadcasted_iota(jnp.int32, (nrow, LANES), 1)
    rowc = lax.broadcasted_iota(jnp.int32, (nrow, 1), 0)
    tq = past_len + rowc // n_heads
    slope = slope_ref[...]

    @pl.when(n == 0)
    def _():
        q_scr[...] = q_ref[...].astype(BF16)
        g_scr[...] = jnp.zeros(g_scr.shape, F32)
        m_scr[...] = jnp.zeros(m_scr.shape, F32)
        l_scr[...] = jnp.zeros(l_scr.shape, F32)

    qb = q_scr[...]

    def masked_scores(keys_bf, pos0, width, extra_ok=None):
        lane = lax.broadcasted_iota(jnp.int32, (nrow, width), 1)
        kpos = pos0 + lane // n_heads
        s = lax.dot_general(qb, keys_bf, _NT, preferred_element_type=F32) * scale
        s = s - slope * (tq - kpos).astype(F32)
        ok = (lane % n_heads) == (rowc % n_heads)
        if extra_ok is not None:
            ok = ok & extra_ok(lane, kpos)
        return jnp.where(ok, s, NEG)

    k0 = k0_ref[...]
    k1 = k1_ref[...]
    s0 = masked_scores(k0.reshape(pkeys, HEAD_DIM).astype(BF16), n * MOBA_BLOCK, pkeys)
    s1 = masked_scores(k1.reshape(pkeys, HEAD_DIM).astype(BF16), n * MOBA_BLOCK + page, pkeys)
    m_n = jnp.maximum(jnp.max(s0, axis=1, keepdims=True), jnp.max(s1, axis=1, keepdims=True))
    p0 = jnp.exp(s0 - m_n)
    p1 = jnp.exp(s1 - m_n)
    l_n = jnp.sum(p0, axis=1, keepdims=True) + jnp.sum(p1, axis=1, keepdims=True)
    out_n = (jnp.dot(p0.astype(BF16), v0_ref[...].reshape(pkeys, HEAD_DIM).astype(BF16),
                     preferred_element_type=F32)
             + jnp.dot(p1.astype(BF16), v1_ref[...].reshape(pkeys, HEAD_DIM).astype(BF16),
                       preferred_element_type=F32))
    kmean = ((jnp.sum(k0, axis=0) + jnp.sum(k1, axis=0)) / MOBA_BLOCK).astype(BF16).astype(F32)
    g_n = jnp.sum(qb.astype(F32) * jnp.concatenate([kmean] * n_q, axis=0), axis=1, keepdims=True)
    here = lane128 == n
    g_scr[...] = jnp.where(here, g_n, g_scr[...])
    m_scr[...] = jnp.where(here, m_n, m_scr[...])
    l_scr[...] = jnp.where(here, l_n, l_scr[...])
    out_scr[n] = out_n

    @pl.when(n == n_blocks - 1)
    def _():
        sel = _topk_select(g_scr[...], lane128 < n_blocks, n_blocks, axis=1)
        pad = jnp.zeros((LANES - nrow, HEAD_DIM), F32)
        ownk_scr[...] = jnp.concatenate([kn_ref[...], pad], axis=0).astype(BF16)
        ownv_scr[...] = jnp.concatenate([vn_ref[...], pad], axis=0).astype(BF16)
        s_own = masked_scores(ownk_scr[...], past_len, LANES,
                              lambda lane, kpos: (lane < nrow) & (kpos <= tq))
        m_blk = m_scr[...]
        m_fin = jnp.maximum(jnp.max(jnp.where(sel, m_blk, NEG), axis=1, keepdims=True),
                            jnp.max(s_own, axis=1, keepdims=True))
        w = jnp.where(sel, jnp.exp(jnp.where(sel, m_blk - m_fin, 0.0)), 0.0)
        p_own = jnp.exp(s_own - m_fin)
        l_fin = jnp.sum(w * l_scr[...], axis=1, keepdims=True) + jnp.sum(p_own, axis=1, keepdims=True)
        comb = jnp.dot(p_own.astype(BF16), ownv_scr[...], preferred_element_type=F32)
        for nn in range(n_blocks):
            comb = comb + w[:, nn:nn + 1] * out_scr[nn]
        o_ref[...] = (comb / l_fin).astype(o_ref.dtype)


def _moba_sample(q, k_new, v_new, cache_k, cache_v, layer, page_table, slopes_np, past_len):
    db, nrow, hd = q.shape
    page, n_heads = cache_k.shape[2:4]
    n_q = nrow // n_heads
    n_pages = page_table.shape[1]
    n_blocks = past_len // MOBA_BLOCK
    assert past_len % MOBA_BLOCK == 0 and MOBA_BLOCK == 2 * page and n_pages * page == past_len
    assert n_blocks >= MOBA_TOPK and n_blocks <= LANES and n_heads == SUBLANES
    assert nrow % 16 == 0 and nrow <= LANES
    slope_col = np.tile(slopes_np, n_q).reshape(nrow, 1)
    seqspec = pl.BlockSpec((None, nrow, hd), lambda s, n, pt: (s, 0, 0))

    def pagespec(which):
        return pl.BlockSpec((None, None, page, n_heads, hd),
                            lambda s, n, pt: (layer, pt[s * n_pages + 2 * n + which], 0, 0, 0))

    grid_spec = pltpu.PrefetchScalarGridSpec(
        num_scalar_prefetch=1,
        grid=(db, n_blocks),
        in_specs=[seqspec, seqspec, seqspec, pl.BlockSpec((nrow, 1), lambda s, n, pt: (0, 0)),
                  pagespec(0), pagespec(1), pagespec(0), pagespec(1)],
        out_specs=seqspec,
        scratch_shapes=[pltpu.VMEM((nrow, hd), BF16),
                        pltpu.VMEM((nrow, LANES), F32), pltpu.VMEM((nrow, LANES), F32),
                        pltpu.VMEM((nrow, LANES), F32),
                        pltpu.VMEM((n_blocks, nrow, hd), F32),
                        pltpu.VMEM((LANES, hd), BF16), pltpu.VMEM((LANES, hd), BF16)],
    )
    return pl.pallas_call(
        functools.partial(_moba_sample_kernel, n_heads=n_heads, n_q=n_q, past_len=past_len, n_blocks=n_blocks),
        grid_spec=grid_spec,
        out_shape=jax.ShapeDtypeStruct((db, nrow, hd), F32),
        compiler_params=_cparams(("arbitrary", "arbitrary")),
        name="moba_sample",
    )(page_table.reshape(-1), q, k_new, v_new, jnp.asarray(slope_col), cache_k, cache_k, cache_v, cache_v)


def _outproj_kernel(oa_ref, u_ref, vg_ref, x_ref, gt_ref, sc_ref, sh_ref, ga_ref, gg_ref, gf_ref,
                    wo_ref, ws_ref, bs_ref, x1_ref, h2_ref, og_scr, *, time_major, n_groups):
    tm = x_ref.shape[0]
    da = oa_ref.shape[1]
    if time_major:
        n_t = ws_ref.shape[0]
        slab = tm // n_t
        for t in range(n_t):
            mixed = jnp.zeros((slab, da), F32) + bs_ref[t:t + 1, :]
            for s in range(t + 1):
                mixed = mixed + ws_ref[t, s:s + 1, :] * vg_ref[s * slab:(s + 1) * slab, :].astype(F32)
            og_scr[t * slab:(t + 1) * slab, :] = u_ref[t * slab:(t + 1) * slab, :] * mixed
    else:
        tri = (lax.broadcasted_iota(jnp.int32, (CHUNK, CHUNK), 0)
               >= lax.broadcasted_iota(jnp.int32, (CHUNK, CHUNK), 1))
        for gi in range(n_groups):
            sl = slice(gi * GROUP_DIM, (gi + 1) * GROUP_DIM)
            w = jnp.where(tri, ws_ref[gi], 0.0).astype(BF16)
            for c in range(tm // CHUNK):
                rs = slice(c * CHUNK, (c + 1) * CHUNK)
                mixed = jnp.dot(w, vg_ref[rs, sl].astype(BF16), preferred_element_type=F32) + bs_ref[:, sl]
                og_scr[rs, sl] = u_ref[rs, sl] * mixed
    a_n = _rms(oa_ref[...], ga_ref[...]).astype(BF16)
    g_n = _rms(og_scr[...], gg_ref[...]).astype(BF16)
    out = (jnp.dot(a_n, wo_ref[0:da, :], preferred_element_type=F32)
           + jnp.dot(g_n, wo_ref[da:, :], preferred_element_type=F32))
    x1 = x_ref[...] + _tile_rows(gt_ref[...], tm) * out
    x1_ref[...] = x1
    h2 = _rms(x1, gf_ref[...]) * (1.0 + _tile_rows(sc_ref[...], tm)) + _tile_rows(sh_ref[...], tm)
    h2_ref[...] = h2.astype(h2_ref.dtype)


def _outproj(oa, u, vg, x, mod, g_oa, g_og, g_ffn, w_out_bf, ws, bs, *, tm, tiles_per_batch, time_major):
    rows, d = x.shape
    da = oa.shape[1]
    r = mod.shape[1]
    modspec = lambda m: pl.BlockSpec((None, r, d), lambda i, m=m: (i // tiles_per_batch, 0, m))
    row = lambda n: pl.BlockSpec((tm, n), lambda i: (i, 0))
    full = lambda a: pl.BlockSpec(a.shape, lambda i, nd=a.ndim: (0,) * nd)
    g_oa, g_og, g_ffn = g_oa.reshape(1, -1), g_og.reshape(1, -1), g_ffn.reshape(1, -1)
    return pl.pallas_call(
        functools.partial(_outproj_kernel, time_major=time_major, n_groups=da // GROUP_DIM),
        grid=(rows // tm,),
        in_specs=[row(da), row(da), row(da), row(d), modspec(2), modspec(4), modspec(3),
                  full(g_oa), full(g_og), full(g_ffn), full(w_out_bf), full(ws), full(bs)],
        out_specs=[row(d), row(d)],
        out_shape=[jax.ShapeDtypeStruct((rows, d), F32), jax.ShapeDtypeStruct((rows, d), BF16)],
        scratch_shapes=[pltpu.VMEM((tm, da), F32)],
        compiler_params=_cparams(("arbitrary",)),
        name="gmlp_out_proj",
    )(oa, u, vg, x, mod, mod, mod, g_oa, g_og, g_ffn, w_out_bf, ws, bs)


def _ffn_kernel(h_ref, x1_ref, gt_ref, wa_ref, wp_ref, wc_ref, bc_ref, wd_ref, prev_ref,
                y_ref, conv_ref, abuf, halo, acc, *, shift, tiles_per_batch):
    i = pl.program_id(0)
    f = pl.program_id(1)
    nf = pl.num_programs(1)
    tm = h_ref.shape[0]
    keep = (CONV_W - 1) * shift
    off = abuf.shape[0] - tm

    h = h_ref[...]
    a = jnp.dot(h, wa_ref[...], preferred_element_type=F32)
    p = jnp.dot(h, wp_ref[...], preferred_element_type=F32)

    @pl.when(i % tiles_per_batch == 0)
    def _():
        abuf[off - keep:off, :] = prev_ref[...]

    @pl.when(i % tiles_per_batch != 0)
    def _():
        abuf[off - keep:off, :] = halo[f]

    abuf[off:off + tm, :] = a
    ac = bc_ref[...] + wc_ref[CONV_W - 1:CONV_W, :] * a
    for jj in range(CONV_W - 1):
        back = (CONV_W - 1 - jj) * shift
        ac = ac + wc_ref[jj:jj + 1, :] * abuf[off - back:off - back + tm, :]
    last = a[tm - keep:, :]
    halo[f] = last
    conv_ref[...] = last
    gate = (ac * jax.nn.sigmoid(ac)) * p
    part = jnp.dot(gate.astype(BF16), wd_ref[...], preferred_element_type=F32)

    @pl.when(f == 0)
    def _():
        acc[...] = part

    @pl.when(f != 0)
    def _():
        acc[...] += part

    @pl.when(f == nf - 1)
    def _():
        y_ref[...] = x1_ref[...] + _tile_rows(gt_ref[...], tm) * acc[...]


def _ffn(h2, x1, mod, w_up_bf, w_conv, b_conv, w_down_bf, conv_prev, *, tm, tf, tiles_per_batch, shift):
    rows, d = x1.shape
    ff = w_down_bf.shape[0]
    nf = ff // tf
    keep = conv_prev.shape[1]
    r = mod.shape[1]
    off = -(-keep // SUBLANES) * SUBLANES
    n_tiles = rows // tm
    y, tail = pl.pallas_call(
        functools.partial(_ffn_kernel, shift=shift, tiles_per_batch=tiles_per_batch),
        grid=(n_tiles, nf),
        in_specs=[pl.BlockSpec((tm, d), lambda i, f: (i, 0)),
                  pl.BlockSpec((tm, d), lambda i, f: (i, 0)),
                  pl.BlockSpec((None, r, d), lambda i, f: (i // tiles_per_batch, 0, 5)),
                  pl.BlockSpec((d, tf), lambda i, f: (0, f)),
                  pl.BlockSpec((d, tf), lambda i, f: (0, f + nf)),
                  pl.BlockSpec((CONV_W, tf), lambda i, f: (0, f)),
                  pl.BlockSpec((1, tf), lambda i, f: (0, f)),
                  pl.BlockSpec((tf, d), lambda i, f: (f, 0)),
                  pl.BlockSpec((None, keep, tf), lambda i, f: (i // tiles_per_batch, 0, f))],
        out_specs=[pl.BlockSpec((tm, d), lambda i, f: (i, 0)),
                   pl.BlockSpec((None, keep, tf), lambda i, f: (i, 0, f))],
        out_shape=[jax.ShapeDtypeStruct((rows, d), F32), jax.ShapeDtypeStruct((n_tiles, keep, ff), F32)],
        scratch_shapes=[pltpu.VMEM((off + tm, tf), F32), pltpu.VMEM((nf, keep, tf), F32),
                        pltpu.VMEM((tm, d), F32)],
        compiler_params=_cparams(("arbitrary", "arbitrary")),
        name="conv_ffn",
    )(h2, x1, mod, w_up_bf, w_up_bf, w_conv, b_conv.reshape(1, ff), w_down_bf, conv_prev)
    return y, tail[tiles_per_batch - 1::tiles_per_batch]


def kernel(x_prompt, x_sample, cache_k, cache_v, state_conv, page_table, c_prompt, c_sample, w_ada, b_ada, g_norm_mix, w_in, g_q, g_k, g_v, w_s, b_s, g_out_attn, g_out_gmlp, w_out, g_norm_ffn, w_up, w_conv, b_conv, w_down):
    depth = w_ada.shape[0]
    b, s, d = x_prompt.shape
    db, n_q, _ = x_sample.shape
    page, n_heads, hd = cache_k.shape[2:]
    da = n_heads * hd
    ff = w_down.shape[1]
    n_groups = w_s.shape[1]
    past_len = page_table.shape[1] * page
    assert hd == HEAD_DIM and da == n_groups * GROUP_DIM and s % CHUNK == 0 and n_q <= CHUNK
    assert w_in.shape[2] == 5 * da
    slopes_np = (2.0 ** (-8.0 * np.arange(1, n_heads + 1) / n_heads)).astype(np.float32)
    slopes = jnp.asarray(slopes_np)

    n_c = b + db
    pad_c = -(-n_c // SUBLANES) * SUBLANES
    c_all = jnp.concatenate([c_prompt, c_sample, jnp.zeros((pad_c - n_c, d), F32)], axis=0)

    h_p = x_prompt.reshape(b * s, d)
    h_s = x_sample.transpose(1, 0, 2).reshape(n_q * db, d)
    tm_p = 512
    outs = [[] for _ in range(7)]
    for layer in range(depth):
        w_ada_bf = w_ada[layer].astype(BF16)
        w_in_bf = w_in[layer].astype(BF16)
        w_out_bf = w_out[layer].astype(BF16)
        w_up_bf = w_up[layer].astype(BF16)
        w_down_bf = w_down[layer].astype(BF16)

        mod = _ada(c_all, w_ada_bf, b_ada[layer])
        mod_p = mod[:b].reshape(b, 1, N_MOD * d)
        mod_s = mod[b:n_c].reshape(1, db, N_MOD * d)

        q, k, v, u, vg = _inproj(h_p, mod_p, g_norm_mix[layer], w_in_bf, g_q[layer], g_k[layer], g_v[layer],
                                 tm=tm_p, tiles_per_batch=s // tm_p, q_dtype=BF16, vg_dtype=BF16)
        o_attn = _moba_prompt(q.reshape(b, s, da), k.reshape(b, s, da), v.reshape(b, s, da), slopes)
        bs_rows = jnp.repeat(b_s[layer].T, GROUP_DIM, axis=1)
        x1, h2 = _outproj(o_attn.reshape(b * s, da), u, vg, h_p, mod_p, g_out_attn[layer], g_out_gmlp[layer],
                          g_norm_ffn[layer], w_out_bf, w_s[layer], bs_rows,
                          tm=256, tiles_per_batch=s // 256, time_major=False)
        conv0 = jnp.zeros((b, CONV_W - 1, ff), F32)
        y_p, conv_p = _ffn(h2, x1, mod_p, w_up_bf, w_conv[layer], b_conv[layer], w_down_bf, conv0,
                           tm=tm_p, tf=512, tiles_per_batch=s // tm_p, shift=1)

        qs, ks, vs, us, vgs = _inproj(h_s, mod_s, g_norm_mix[layer], w_in_bf, g_q[layer], g_k[layer], g_v[layer],
                                      tm=db, tiles_per_batch=n_q, q_dtype=F32, vg_dtype=F32)
        seq_major = lambda a: a.reshape(n_q, db, -1).transpose(1, 0, 2)
        per_head = lambda a: seq_major(a).reshape(db, n_q * n_heads, hd)
        ks_h, vs_h = per_head(ks), per_head(vs)
        o_s = _moba_sample(per_head(qs), ks_h, vs_h, cache_k, cache_v, layer, page_table, slopes_np, past_len)
        o_s = o_s.reshape(db, n_q, da).transpose(1, 0, 2).reshape(n_q * db, da)
        ws_t = jnp.repeat(jnp.tril(w_s[layer][:, :n_q, :n_q]).transpose(1, 2, 0), GROUP_DIM, axis=2)
        bs_t = jnp.repeat(b_s[layer][:, :n_q].T, GROUP_DIM, axis=1)
        x1s, h2s = _outproj(o_s, us, vgs, h_s, mod_s, g_out_attn[layer], g_out_gmlp[layer], g_norm_ffn[layer],
                            w_out_bf, ws_t, bs_t, tm=n_q * db, tiles_per_batch=1, time_major=True)
        prev_s = state_conv[layer].transpose(1, 0, 2).reshape(1, (CONV_W - 1) * db, ff)
        y_s, conv_s = _ffn(h2s, x1s, mod_s, w_up_bf, w_conv[layer], b_conv[layer], w_down_bf, prev_s,
                           tm=n_q * db, tf=512, tiles_per_batch=1, shift=db)

        h_p, h_s = y_p, y_s
        outs[0].append(k.reshape(b, s // page, page, n_heads, hd))
        outs[1].append(v.reshape(b, s // page, page, n_heads, hd))
        outs[2].append(ks_h.reshape(db, n_q, n_heads, hd))
        outs[3].append(vs_h.reshape(db, n_q, n_heads, hd))
        outs[4].append(conv_p)
        outs[5].append(conv_s.reshape(CONV_W - 1, db, ff).transpose(1, 0, 2))
        outs[6].append(seq_major(vgs))

    y_prompt = h_p.reshape(b, s, d)
    y_sample = h_s.reshape(n_q, db, d).transpose(1, 0, 2)
    return (y_prompt, y_sample) + tuple(jnp.stack(o) for o in outs)
```

```python
import functools
import math

import numpy as np
import jax
import jax.numpy as jnp
from jax import lax
from jax.experimental import pallas as pl
from jax.experimental.pallas import tpu as pltpu

F32 = jnp.float32
BF16 = jnp.bfloat16

HEAD_DIM = 128
GROUP_DIM = 128
MOBA_BLOCK = 256
MOBA_TOPK = 3
CHUNK = 128
CONV_W = 3
N_MOD = 6
EPS = 1e-6
NEG = -1e30
LOG2E = math.log2(math.e)
LANES = 128
SUBLANES = 8
VMEM_LIMIT = 56 * 1024 * 1024

_NT = (((1,), (1,)), ((), ()))


def _cparams(sem):
    return pltpu.CompilerParams(dimension_semantics=sem, vmem_limit_bytes=VMEM_LIMIT)


def _rms(x, g):
    ms = jnp.mean(x * x, axis=-1, keepdims=True)
    return (x * lax.rsqrt(ms + EPS)) * g


def _tile_rows(m, rows):
    r = m.shape[0]
    if r == 1 or r == rows:
        return m
    return jnp.concatenate([m] * (rows // r), axis=0)


def _topk_select(g, valid, n_blocks, axis):
    blk = lax.broadcasted_iota(jnp.int32, g.shape, axis)
    gm = jnp.where(valid, g, -jnp.inf)
    rank = jnp.zeros(g.shape, jnp.int32)
    for m in range(n_blocks):
        one = gm[m:m + 1, :] if axis == 0 else gm[:, m:m + 1]
        beats = (one > gm) | ((one == gm) & (m < blk))
        rank = rank + beats.astype(jnp.int32)
    return valid & (rank < MOBA_TOPK)


def _ada_kernel(c_ref, w_ref, b_ref, o_ref):
    c = c_ref[...]
    s = (c * jax.nn.sigmoid(c)).astype(BF16)
    o_ref[...] = jnp.dot(s, w_ref[...].astype(BF16), preferred_element_type=F32) + b_ref[...]


def _ada(c, w, b):
    m, d = c.shape
    n = w.shape[1]
    tn = 1024
    return pl.pallas_call(
        _ada_kernel,
        grid=(n // tn,),
        in_specs=[pl.BlockSpec((m, d), lambda j: (0, 0)),
                  pl.BlockSpec((d, tn), lambda j: (0, j)),
                  pl.BlockSpec((1, tn), lambda j: (0, j))],
        out_specs=pl.BlockSpec((m, tn), lambda j: (0, j)),
        out_shape=jax.ShapeDtypeStruct((m, n), F32),
        compiler_params=_cparams(("arbitrary",)),
        name="ada_mod",
    )(c, w, b.reshape(1, n))


def _inproj_kernel(x_ref, sc_ref, sh_ref, gn_ref, w_ref, gq_ref, gk_ref, gv_ref,
                   q_ref, k_ref, v_ref, u_ref, vg_ref, h_scr, *, n_heads):
    tm = x_ref.shape[0]
    y = _rms(x_ref[...], gn_ref[...])
    h = y * (1.0 + _tile_rows(sc_ref[...], tm)) + _tile_rows(sh_ref[...], tm)
    h_scr[...] = h.astype(BF16)

    tn = n_heads * HEAD_DIM
    cw = 2 * HEAD_DIM
    for part, out_ref in enumerate((q_ref, k_ref, v_ref, u_ref, vg_ref)):
        for c0 in range(0, tn, cw):
            acc = jnp.dot(h_scr[...], w_ref[:, part * tn + c0:part * tn + c0 + cw], preferred_element_type=F32)
            for hh in range(cw // HEAD_DIM):
                sl = slice(c0 + hh * HEAD_DIM, c0 + (hh + 1) * HEAD_DIM)
                a = acc[:, hh * HEAD_DIM:(hh + 1) * HEAD_DIM]
                if part == 0:
                    a = _rms(a, gq_ref[...])
                elif part == 1:
                    a = _rms(a, gk_ref[...])
                elif part == 3:
                    a = jax.nn.gelu(a)
                elif part == 4:
                    a = _rms(jax.nn.gelu(a), gv_ref[:, sl])
                out_ref[:, sl] = a.astype(out_ref.dtype)


def _inproj(x, mod, g_norm, w_in_bf, g_q, g_k, g_v, *, tm, tiles_per_batch, q_dtype, vg_dtype):
    rows, d = x.shape
    n_tot = w_in_bf.shape[1]
    tn = n_tot // 5
    n_heads = tn // HEAD_DIM
    r = mod.shape[1]
    modspec = lambda m: pl.BlockSpec((None, r, d), lambda i, m=m: (i // tiles_per_batch, 0, m))
    rowspec = pl.BlockSpec((tm, tn), lambda i: (i, 0))
    vec = lambda n: pl.BlockSpec((1, n), lambda i: (0, 0))
    outs = pl.pallas_call(
        functools.partial(_inproj_kernel, n_heads=n_heads),
        grid=(rows // tm,),
        in_specs=[pl.BlockSpec((tm, d), lambda i: (i, 0)),
                  modspec(1), modspec(0), vec(d),
                  pl.BlockSpec((d, n_tot), lambda i: (0, 0), pipeline_mode=pl.Buffered(1)),
                  vec(HEAD_DIM), vec(HEAD_DIM), vec(tn)],
        out_specs=[rowspec] * 5,
        out_shape=[jax.ShapeDtypeStruct((rows, tn), dt) for dt in (q_dtype, F32, F32, F32, vg_dtype)],
        scratch_shapes=[pltpu.VMEM((tm, d), BF16)],
        compiler_params=_cparams(("arbitrary",)),
        name="in_proj",
    )(x, mod, mod, g_norm.reshape(1, d), w_in_bf, g_q.reshape(1, -1), g_k.reshape(1, -1), g_v.reshape(1, -1))
    return outs


def _moba_prompt_kernel(slopes_ref, q_ref, k_ref, v_ref, o_ref, kb_scr, vb_scr, km_scr, al_scr,
                        *, n_blocks, heads_per_step):
    hg = pl.program_id(1)
    qi = pl.program_id(2)
    blk = MOBA_BLOCK
    tile = 2 * blk
    c1 = HEAD_DIM ** -0.5 * LOG2E
    heads = range(heads_per_step)
    nslope = [-slopes_ref[hg * heads_per_step + e] * LOG2E for e in heads]

    @pl.when(qi == 0)
    def _():
        km_scr[...] = jnp.zeros(km_scr.shape, F32)
        lane = lax.broadcasted_iota(jnp.int32, (blk, LANES), 1)
        rel = (lax.broadcasted_iota(jnp.int32, (blk, tile), 0)
               - lax.broadcasted_iota(jnp.int32, (blk, tile), 1)).astype(F32)
        for e in heads:
            cols = slice(e * HEAD_DIM, (e + 1) * HEAD_DIM)
            for n in range(n_blocks):
                rows = slice(n * blk, (n + 1) * blk)
                kblk = k_ref[rows, cols]
                kb_scr[e, rows, 0:HEAD_DIM] = kblk.astype(BF16)
                kb_scr[e, rows, HEAD_DIM:] = jnp.where(lane == n, 1.0, 0.0).astype(BF16)
                vb_scr[e, rows, :] = v_ref[rows, cols].astype(BF16)
                km_scr[e, n:n + 1, :] = jnp.mean(kblk, axis=0, keepdims=True)
            al_scr[e] = nslope[e] * rel

    def augmented_query(e):
        q = q_ref[:, e * HEAD_DIM:(e + 1) * HEAD_DIM]
        g = lax.dot_general(km_scr[e].astype(BF16), q, _NT, preferred_element_type=F32)[:n_blocks]
        bidx = lax.broadcasted_iota(jnp.int32, g.shape, 0)
        sel = _topk_select(g, bidx < qi, n_blocks, axis=0)
        bias = jnp.where(sel | (bidx == qi), 0.0, NEG).astype(F32)
        bias = jnp.concatenate([bias, jnp.full((LANES - n_blocks, blk), NEG, F32)], axis=0)
        return jnp.concatenate([q, bias.T.astype(BF16)], axis=1)

    q_aug = [augmented_query(e) for e in heads]

    def tile_scores(e, t):
        start = pl.multiple_of(t * tile, tile)
        raw = lax.dot_general(q_aug[e], kb_scr[e, pl.ds(start, tile), :], _NT, preferred_element_type=F32)
        off = nslope[e] * ((qi - 2 * t) * blk).astype(F32)
        return raw * c1 + al_scr[e], off, start

    t_own = qi // 2
    rowi = lax.broadcasted_iota(jnp.int32, (blk, tile), 0)
    coli = lax.broadcasted_iota(jnp.int32, (blk, tile), 1)
    causal = coli <= rowi + (qi - 2 * t_own) * blk
    carry = []
    for e in heads:
        s, off, start = tile_scores(e, t_own)
        s = jnp.where(causal, s, NEG)
        m = jnp.max(s, axis=1, keepdims=True) + off
        p = jnp.exp2(s - (m - off))
        l = jnp.sum(p, axis=1, keepdims=True)
        acc = jnp.dot(p.astype(BF16), vb_scr[e, pl.ds(start, tile), :], preferred_element_type=F32)
        carry += [m, l, acc]

    def body(t, carry):
        out = []
        for e in heads:
            m, l, acc = carry[3 * e:3 * e + 3]
            s, off, start = tile_scores(e, t)
            m_new = jnp.maximum(m, jnp.max(s, axis=1, keepdims=True) + off)
            alpha = jnp.exp2(m - m_new)
            p = jnp.exp2(s - (m_new - off))
            l = alpha * l + jnp.sum(p, axis=1, keepdims=True)
            acc = alpha * acc + jnp.dot(p.astype(BF16), vb_scr[e, pl.ds(start, tile), :],
                                        preferred_element_type=F32)
            out += [m_new, l, acc]
        return tuple(out)

    carry = lax.fori_loop(0, t_own, body, tuple(carry))
    for e in heads:
        _, l, acc = carry[3 * e:3 * e + 3]
        o_ref[:, e * HEAD_DIM:(e + 1) * HEAD_DIM] = (acc / l).astype(o_ref.dtype)


def _moba_prompt(q, k, v, slopes, heads_per_step=2):
    b, s, da = k.shape
    n_heads = da // HEAD_DIM
    n_blocks = s // MOBA_BLOCK
    hw = heads_per_step * HEAD_DIM
    assert s % (2 * MOBA_BLOCK) == 0 and n_blocks >= MOBA_TOPK and n_blocks <= LANES
    assert n_heads % heads_per_step == 0
    qspec = pl.BlockSpec((None, MOBA_BLOCK, hw), lambda bb, hh, qi: (bb, qi, hh))
    kvspec = pl.BlockSpec((None, s, hw), lambda bb, hh, qi: (bb, 0, hh))
    return pl.pallas_call(
        functools.partial(_moba_prompt_kernel, n_blocks=n_blocks, heads_per_step=heads_per_step),
        grid=(b, n_heads // heads_per_step, n_blocks),
        in_specs=[pl.BlockSpec(memory_space=pltpu.SMEM), qspec, kvspec, kvspec],
        out_specs=qspec,
        out_shape=jax.ShapeDtypeStruct((b, s, da), F32),
        scratch_shapes=[pltpu.VMEM((heads_per_step, s, 2 * HEAD_DIM), BF16),
                        pltpu.VMEM((heads_per_step, s, HEAD_DIM), BF16),
                        pltpu.VMEM((heads_per_step, LANES, HEAD_DIM), F32),
                        pltpu.VMEM((heads_per_step, MOBA_BLOCK, 2 * MOBA_BLOCK), F32)],
        compiler_params=_cparams(("arbitrary", "arbitrary", "arbitrary")),
        name="moba_prompt",
    )(slopes, q, k, v)


def _moba_sample_kernel(pt_ref, q_ref, kn_ref, vn_ref, slope_ref, *refs, n_heads, n_q, past_len, n_blocks):
    del pt_ref
    n_pages = 2 * n_blocks
    k_refs, v_refs, o_ref = refs[:n_pages], refs[n_pages:2 * n_pages], refs[2 * n_pages]
    nrow = n_q * n_heads
    scale = HEAD_DIM ** -0.5
    page = k_refs[0].shape[0]
    pkeys = page * n_heads
    lane128 = lax.broadcasted_iota(jnp.int32, (nrow, LANES), 1)
    rowc = lax.broadcasted_iota(jnp.int32, (nrow, 1), 0)
    tq = past_len + rowc // n_heads
    slope = slope_ref[...]
    qb = q_ref[...].astype(BF16)
    qf = qb.astype(F32)

    def masked_scores(keys_bf, pos0, width, extra_ok=None):
        lane = lax.broadcasted_iota(jnp.int32, (nrow, width), 1)
        kpos = pos0 + lane // n_heads
        s = lax.dot_general(qb, keys_bf, _NT, preferred_element_type=F32) * scale
        s = s - slope * (tq - kpos).astype(F32)
        ok = (lane % n_heads) == (rowc % n_heads)
        if extra_ok is not None:
            ok = ok & extra_ok(lane, kpos)
        return jnp.where(ok, s, NEG)

    g_all = jnp.zeros((nrow, LANES), F32)
    m_all = jnp.zeros((nrow, LANES), F32)
    l_all = jnp.zeros((nrow, LANES), F32)
    outs = []
    for n in range(n_blocks):
        k0 = k_refs[2 * n][...]
        k1 = k_refs[2 * n + 1][...]
        s0 = masked_scores(k0.reshape(pkeys, HEAD_DIM).astype(BF16), n * MOBA_BLOCK, pkeys)
        s1 = masked_scores(k1.reshape(pkeys, HEAD_DIM).astype(BF16), n * MOBA_BLOCK + page, pkeys)
        m_n = jnp.maximum(jnp.max(s0, axis=1, keepdims=True), jnp.max(s1, axis=1, keepdims=True))
        p0 = jnp.exp(s0 - m_n)
        p1 = jnp.exp(s1 - m_n)
        l_n = jnp.sum(p0, axis=1, keepdims=True) + jnp.sum(p1, axis=1, keepdims=True)
        outs.append(jnp.dot(p0.astype(BF16), v_refs[2 * n][...].reshape(pkeys, HEAD_DIM).astype(BF16),
                            preferred_element_type=F32)
                    + jnp.dot(p1.astype(BF16), v_refs[2 * n + 1][...].reshape(pkeys, HEAD_DIM).astype(BF16),
                              preferred_element_type=F32))
        kmean = ((jnp.sum(k0, axis=0) + jnp.sum(k1, axis=0)) / MOBA_BLOCK).astype(BF16).astype(F32)
        g_n = jnp.sum(qf * jnp.concatenate([kmean] * n_q, axis=0), axis=1, keepdims=True)
        here = lane128 == n
        g_all = jnp.where(here, g_n, g_all)
        m_all = jnp.where(here, m_n, m_all)
        l_all = jnp.where(here, l_n, l_all)

    sel = _topk_select(g_all, lane128 < n_blocks, n_blocks, axis=1)
    pad = jnp.zeros((LANES - nrow, HEAD_DIM), F32)
    own_k = jnp.concatenate([kn_ref[...], pad], axis=0).astype(BF16)
    own_v = jnp.concatenate([vn_ref[...], pad], axis=0).astype(BF16)
    s_own = masked_scores(own_k, past_len, LANES, lambda lane, kpos: (lane < nrow) & (kpos <= tq))
    m_fin = jnp.maximum(jnp.max(jnp.where(sel, m_all, NEG), axis=1, keepdims=True),
                        jnp.max(s_own, axis=1, keepdims=True))
    w = jnp.where(sel, jnp.exp(jnp.where(sel, m_all - m_fin, 0.0)), 0.0)
    p_own = jnp.exp(s_own - m_fin)
    l_fin = jnp.sum(w * l_all, axis=1, keepdims=True) + jnp.sum(p_own, axis=1, keepdims=True)
    comb = jnp.dot(p_own.astype(BF16), own_v, preferred_element_type=F32)
    for n in range(n_blocks):
        comb = comb + w[:, n:n + 1] * outs[n]
    o_ref[...] = (comb / l_fin).astype(o_ref.dtype)


def _moba_sample(q, k_new, v_new, cache_k, cache_v, layer, page_table, slopes_np, past_len):
    db, nrow, hd = q.shape
    page, n_heads = cache_k.shape[2:4]
    n_q = nrow // n_heads
    n_pages = page_table.shape[1]
    n_blocks = past_len // MOBA_BLOCK
    assert past_len % MOBA_BLOCK == 0 and MOBA_BLOCK == 2 * page and n_pages * page == past_len
    assert n_blocks >= MOBA_TOPK and n_blocks <= LANES and n_heads == SUBLANES
    assert nrow % 16 == 0 and nrow <= LANES
    slope_col = np.tile(slopes_np, n_q).reshape(nrow, 1)
    seqspec = pl.BlockSpec((None, nrow, hd), lambda s, pt: (s, 0, 0))
    pagespecs = [pl.BlockSpec((None, None, page, n_heads, hd),
                              lambda s, pt, j=j: (layer, pt[s * n_pages + j], 0, 0, 0))
                 for j in range(n_pages)]
    grid_spec = pltpu.PrefetchScalarGridSpec(
        num_scalar_prefetch=1,
        grid=(db,),
        in_specs=[seqspec, seqspec, seqspec, pl.BlockSpec((nrow, 1), lambda s, pt: (0, 0))] + pagespecs * 2,
        out_specs=seqspec,
    )
    return pl.pallas_call(
        functools.partial(_moba_sample_kernel, n_heads=n_heads, n_q=n_q, past_len=past_len, n_blocks=n_blocks),
        grid_spec=grid_spec,
        out_shape=jax.ShapeDtypeStruct((db, nrow, hd), F32),
        compiler_params=_cparams(("arbitrary",)),
        name="moba_sample",
    )(page_table.reshape(-1), q, k_new, v_new, jnp.asarray(slope_col),
      *([cache_k] * n_pages), *([cache_v] * n_pages))


def _outproj_kernel(oa_ref, u_ref, vg_ref, x_ref, gt_ref, sc_ref, sh_ref, ga_ref, gg_ref, gf_ref,
                    wo_ref, ws_ref, bs_ref, x1_ref, h2_ref, og_scr, *, time_major, n_groups):
    tm = x_ref.shape[0]
    da = oa_ref.shape[1]
    if time_major:
        n_t = ws_ref.shape[0]
        slab = tm // n_t
        for t in range(n_t):
            mixed = jnp.zeros((slab, da), F32) + bs_ref[t:t + 1, :]
            for s in range(t + 1):
                mixed = mixed + ws_ref[t, s:s + 1, :] * vg_ref[s * slab:(s + 1) * slab, :].astype(F32)
            og_scr[t * slab:(t + 1) * slab, :] = u_ref[t * slab:(t + 1) * slab, :] * mixed
    else:
        tri = (lax.broadcasted_iota(jnp.int32, (CHUNK, CHUNK), 0)
               >= lax.broadcasted_iota(jnp.int32, (CHUNK, CHUNK), 1))
        for gi in range(n_groups):
            sl = slice(gi * GROUP_DIM, (gi + 1) * GROUP_DIM)
            w = jnp.where(tri, ws_ref[gi], 0.0).astype(BF16)
            for c in range(tm // CHUNK):
                rs = slice(c * CHUNK, (c + 1) * CHUNK)
                mixed = jnp.dot(w, vg_ref[rs, sl].astype(BF16), preferred_element_type=F32) + bs_ref[:, sl]
                og_scr[rs, sl] = u_ref[rs, sl] * mixed
    a_n = _rms(oa_ref[...], ga_ref[...]).astype(BF16)
    g_n = _rms(og_scr[...], gg_ref[...]).astype(BF16)
    out = (jnp.dot(a_n, wo_ref[0:da, :], preferred_element_type=F32)
           + jnp.dot(g_n, wo_ref[da:, :], preferred_element_type=F32))
    x1 = x_ref[...] + _tile_rows(gt_ref[...], tm) * out
    x1_ref[...] = x1
    h2 = _rms(x1, gf_ref[...]) * (1.0 + _tile_rows(sc_ref[...], tm)) + _tile_rows(sh_ref[...], tm)
    h2_ref[...] = h2.astype(h2_ref.dtype)


def _outproj(oa, u, vg, x, mod, g_oa, g_og, g_ffn, w_out_bf, ws, bs, *, tm, tiles_per_batch, time_major):
    rows, d = x.shape
    da = oa.shape[1]
    r = mod.shape[1]
    modspec = lambda m: pl.BlockSpec((None, r, d), lambda i, m=m: (i // tiles_per_batch, 0, m))
    row = lambda n: pl.BlockSpec((tm, n), lambda i: (i, 0))
    full = lambda a: pl.BlockSpec(a.shape, lambda i, nd=a.ndim: (0,) * nd, pipeline_mode=pl.Buffered(1))
    g_oa, g_og, g_ffn = g_oa.reshape(1, -1), g_og.reshape(1, -1), g_ffn.reshape(1, -1)
    return pl.pallas_call(
        functools.partial(_outproj_kernel, time_major=time_major, n_groups=da // GROUP_DIM),
        grid=(rows // tm,),
        in_specs=[row(da), row(da), row(da), row(d), modspec(2), modspec(4), modspec(3),
                  full(g_oa), full(g_og), full(g_ffn), full(w_out_bf), full(ws), full(bs)],
        out_specs=[row(d), row(d)],
        out_shape=[jax.ShapeDtypeStruct((rows, d), F32), jax.ShapeDtypeStruct((rows, d), BF16)],
        scratch_shapes=[pltpu.VMEM((tm, da), F32)],
        compiler_params=_cparams(("arbitrary",)),
        name="gmlp_out_proj",
    )(oa, u, vg, x, mod, mod, mod, g_oa, g_og, g_ffn, w_out_bf, ws, bs)


def _ffn_kernel(h_ref, x1_ref, gt_ref, wa_ref, wp_ref, wc_ref, bc_ref, wd_ref, prev_ref,
                y_ref, conv_ref, abuf, halo, acc, *, shift, tiles_per_batch):
    i = pl.program_id(0)
    f = pl.program_id(1)
    nf = pl.num_programs(1)
    tm = h_ref.shape[0]
    keep = (CONV_W - 1) * shift
    off = abuf.shape[0] - tm

    h = h_ref[...]
    a = jnp.dot(h, wa_ref[...], preferred_element_type=F32)
    p = jnp.dot(h, wp_ref[...], preferred_element_type=F32)

    @pl.when(i % tiles_per_batch == 0)
    def _():
        abuf[off - keep:off, :] = prev_ref[...]

    @pl.when(i % tiles_per_batch != 0)
    def _():
        abuf[off - keep:off, :] = halo[f]

    abuf[off:off + tm, :] = a
    ac = bc_ref[...] + wc_ref[CONV_W - 1:CONV_W, :] * a
    for jj in range(CONV_W - 1):
        back = (CONV_W - 1 - jj) * shift
        ac = ac + wc_ref[jj:jj + 1, :] * abuf[off - back:off - back + tm, :]
    last = a[tm - keep:, :]
    halo[f] = last
    conv_ref[...] = last
    gate = (ac * jax.nn.sigmoid(ac)) * p
    part = jnp.dot(gate.astype(BF16), wd_ref[...], preferred_element_type=F32)

    @pl.when(f == 0)
    def _():
        acc[...] = part

    @pl.when(f != 0)
    def _():
        acc[...] += part

    @pl.when(f == nf - 1)
    def _():
        y_ref[...] = x1_ref[...] + _tile_rows(gt_ref[...], tm) * acc[...]


def _ffn(h2, x1, mod, w_up_bf, w_conv, b_conv, w_down_bf, conv_prev, *, tm, tf, tiles_per_batch, shift):
    rows, d = x1.shape
    ff = w_down_bf.shape[0]
    nf = ff // tf
    keep = conv_prev.shape[1]
    r = mod.shape[1]
    off = -(-keep // SUBLANES) * SUBLANES
    n_tiles = rows // tm
    y, tail = pl.pallas_call(
        functools.partial(_ffn_kernel, shift=shift, tiles_per_batch=tiles_per_batch),
        grid=(n_tiles, nf),
        in_specs=[pl.BlockSpec((tm, d), lambda i, f: (i, 0)),
                  pl.BlockSpec((tm, d), lambda i, f: (i, 0)),
                  pl.BlockSpec((None, r, d), lambda i, f: (i // tiles_per_batch, 0, 5)),
                  pl.BlockSpec((d, tf), lambda i, f: (0, f)),
                  pl.BlockSpec((d, tf), lambda i, f: (0, f + nf)),
                  pl.BlockSpec((CONV_W, tf), lambda i, f: (0, f)),
                  pl.BlockSpec((1, tf), lambda i, f: (0, f)),
                  pl.BlockSpec((tf, d), lambda i, f: (f, 0)),
                  pl.BlockSpec((None, keep, tf), lambda i, f: (i // tiles_per_batch, 0, f))],
        out_specs=[pl.BlockSpec((tm, d), lambda i, f: (i, 0)),
                   pl.BlockSpec((None, keep, tf), lambda i, f: (i, 0, f))],
        out_shape=[jax.ShapeDtypeStruct((rows, d), F32), jax.ShapeDtypeStruct((n_tiles, keep, ff), F32)],
        scratch_shapes=[pltpu.VMEM((off + tm, tf), F32), pltpu.VMEM((nf, keep, tf), F32),
                        pltpu.VMEM((tm, d), F32)],
        compiler_params=_cparams(("arbitrary", "arbitrary")),
        name="conv_ffn",
    )(h2, x1, mod, w_up_bf, w_up_bf, w_conv, b_conv.reshape(1, ff), w_down_bf, conv_prev)
    return y, tail[tiles_per_batch - 1::tiles_per_batch]


def kernel(x_prompt, x_sample, cache_k, cache_v, state_conv, page_table, c_prompt, c_sample, w_ada, b_ada, g_norm_mix, w_in, g_q, g_k, g_v, w_s, b_s, g_out_attn, g_out_gmlp, w_out, g_norm_ffn, w_up, w_conv, b_conv, w_down):
    depth = w_ada.shape[0]
    b, s, d = x_prompt.shape
    db, n_q, _ = x_sample.shape
    page, n_heads, hd = cache_k.shape[2:]
    da = n_heads * hd
    ff = w_down.shape[1]
    n_groups = w_s.shape[1]
    past_len = page_table.shape[1] * page
    assert hd == HEAD_DIM and da == n_groups * GROUP_DIM and s % CHUNK == 0 and n_q <= CHUNK
    assert w_in.shape[2] == 5 * da
    slopes_np = (2.0 ** (-8.0 * np.arange(1, n_heads + 1) / n_heads)).astype(np.float32)
    slopes = jnp.asarray(slopes_np)

    n_c = b + db
    pad_c = -(-n_c // SUBLANES) * SUBLANES
    c_all = jnp.concatenate([c_prompt, c_sample, jnp.zeros((pad_c - n_c, d), F32)], axis=0)

    h_p = x_prompt.reshape(b * s, d)
    h_s = x_sample.transpose(1, 0, 2).reshape(n_q * db, d)
    tm_p = 512
    outs = [[] for _ in range(7)]
    for layer in range(depth):
        w_in_bf = w_in[layer].astype(BF16)
        w_out_bf = w_out[layer].astype(BF16)
        w_up_bf = w_up[layer].astype(BF16)
        w_down_bf = w_down[layer].astype(BF16)

        mod = _ada(c_all, w_ada[layer], b_ada[layer])
        mod_p = mod[:b].reshape(b, 1, N_MOD * d)
        mod_s = mod[b:n_c].reshape(1, db, N_MOD * d)

        q, k, v, u, vg = _inproj(h_p, mod_p, g_norm_mix[layer], w_in_bf, g_q[layer], g_k[layer], g_v[layer],
                                 tm=tm_p, tiles_per_batch=s // tm_p, q_dtype=BF16, vg_dtype=BF16)
        o_attn = _moba_prompt(q.reshape(b, s, da), k.reshape(b, s, da), v.reshape(b, s, da), slopes)
        bs_rows = jnp.repeat(b_s[layer].T, GROUP_DIM, axis=1)
        x1, h2 = _outproj(o_attn.reshape(b * s, da), u, vg, h_p, mod_p, g_out_attn[layer], g_out_gmlp[layer],
                          g_norm_ffn[layer], w_out_bf, w_s[layer], bs_rows,
                          tm=256, tiles_per_batch=s // 256, time_major=False)
        conv0 = jnp.zeros((b, CONV_W - 1, ff), F32)
        y_p, conv_p = _ffn(h2, x1, mod_p, w_up_bf, w_conv[layer], b_conv[layer], w_down_bf, conv0,
                           tm=tm_p, tf=512, tiles_per_batch=s // tm_p, shift=1)

        qs, ks, vs, us, vgs = _inproj(h_s, mod_s, g_norm_mix[layer], w_in_bf, g_q[layer], g_k[layer], g_v[layer],
                                      tm=db, tiles_per_batch=n_q, q_dtype=F32, vg_dtype=F32)
        seq_major = lambda a: a.reshape(n_q, db, -1).transpose(1, 0, 2)
        per_head = lambda a: seq_major(a).reshape(db, n_q * n_heads, hd)
        ks_h, vs_h = per_head(ks), per_head(vs)
        o_s = _moba_sample(per_head(qs), ks_h, vs_h, cache_k, cache_v, layer, page_table, slopes_np, past_len)
        o_s = o_s.reshape(db, n_q, da).transpose(1, 0, 2).reshape(n_q * db, da)
        ws_t = jnp.repeat(jnp.tril(w_s[layer][:, :n_q, :n_q]).transpose(1, 2, 0), GROUP_DIM, axis=2)
        bs_t = jnp.repeat(b_s[layer][:, :n_q].T, GROUP_DIM, axis=1)
        x1s, h2s = _outproj(o_s, us, vgs, h_s, mod_s, g_out_attn[layer], g_out_gmlp[layer], g_norm_ffn[layer],
                            w_out_bf, ws_t, bs_t, tm=n_q * db, tiles_per_batch=1, time_major=True)
        prev_s = state_conv[layer].transpose(1, 0, 2).reshape(1, (CONV_W - 1) * db, ff)
        y_s, conv_s = _ffn(h2s, x1s, mod_s, w_up_bf, w_conv[layer], b_conv[layer], w_down_bf, prev_s,
                           tm=n_q * db, tf=512, tiles_per_batch=1, shift=db)

        h_p, h_s = y_p, y_s
        outs[0].append(k.reshape(b, s // page, page, n_heads, hd))
        outs[1].append(v.reshape(b, s // page, page, n_heads, hd))
        outs[2].append(ks_h.reshape(db, n_q, n_heads, hd))
        outs[3].append(vs_h.reshape(db, n_q, n_heads, hd))
        outs[4].append(conv_p)
        outs[5].append(conv_s.reshape(CONV_W - 1, db, ff).transpose(1, 0, 2))
        outs[6].append(seq_major(vgs))

    y_prompt = h_p.reshape(b, s, d)
    y_sample = h_s.reshape(n_q, db, d).transpose(1, 0, 2)
    return (y_prompt, y_sample) + tuple(jnp.stack(o) for o in outs)
```

```python
import functools
import math

import numpy as np
import jax
import jax.numpy as jnp
from jax import lax
from jax.experimental import pallas as pl
from jax.experimental.pallas import tpu as pltpu

F32 = jnp.float32
BF16 = jnp.bfloat16

HEAD_DIM = 128
GROUP_DIM = 128
MOBA_BLOCK = 256
MOBA_TOPK = 3
CHUNK = 128
CONV_W = 3
N_MOD = 6
EPS = 1e-6
NEG = -1e30
LOG2E = math.log2(math.e)
LANES = 128
SUBLANES = 8
VMEM_LIMIT = 56 * 1024 * 1024

_NT = (((1,), (1,)), ((), ()))


def _cparams(sem):
    return pltpu.CompilerParams(dimension_semantics=sem, vmem_limit_bytes=VMEM_LIMIT)


def _rms(x, g):
    ms = jnp.mean(x * x, axis=-1, keepdims=True)
    return (x * lax.rsqrt(ms + EPS)) * g


def _tile_rows(m, rows):
    r = m.shape[0]
    if r == 1 or r == rows:
        return m
    return jnp.concatenate([m] * (rows // r), axis=0)


def _topk_select(g, valid, n_blocks, axis):
    blk = lax.broadcasted_iota(jnp.int32, g.shape, axis)
    gm = jnp.where(valid, g, -jnp.inf)
    rank = jnp.zeros(g.shape, jnp.int32)
    for m in range(n_blocks):
        one = gm[m:m + 1, :] if axis == 0 else gm[:, m:m + 1]
        beats = (one > gm) | ((one == gm) & (m < blk))
        rank = rank + beats.astype(jnp.int32)
    return valid & (rank < MOBA_TOPK)


def _ada_kernel(c_ref, w_ref, b_ref, o_ref):
    c = c_ref[...]
    s = (c * jax.nn.sigmoid(c)).astype(BF16)
    o_ref[...] = jnp.dot(s, w_ref[...].astype(BF16), preferred_element_type=F32) + b_ref[...]


def _ada(c, w, b):
    m, d = c.shape
    n = w.shape[1]
    tn = 1024
    return pl.pallas_call(
        _ada_kernel,
        grid=(n // tn,),
        in_specs=[pl.BlockSpec((m, d), lambda j: (0, 0)),
                  pl.BlockSpec((d, tn), lambda j: (0, j)),
                  pl.BlockSpec((1, tn), lambda j: (0, j))],
        out_specs=pl.BlockSpec((m, tn), lambda j: (0, j)),
        out_shape=jax.ShapeDtypeStruct((m, n), F32),
        compiler_params=_cparams(("arbitrary",)),
        name="ada_mod",
    )(c, w, b.reshape(1, n))


def _inproj_kernel(x_ref, sc_ref, sh_ref, gn_ref, w_ref, gq_ref, gk_ref, gv_ref,
                   q_ref, k_ref, v_ref, u_ref, vg_ref, h_scr, *, n_heads):
    tm = x_ref.shape[0]
    y = _rms(x_ref[...], gn_ref[...])
    h = y * (1.0 + _tile_rows(sc_ref[...], tm)) + _tile_rows(sh_ref[...], tm)
    h_scr[...] = h.astype(BF16)

    tn = n_heads * HEAD_DIM
    cw = 2 * HEAD_DIM
    for part, out_ref in enumerate((q_ref, k_ref, v_ref, u_ref, vg_ref)):
        for c0 in range(0, tn, cw):
            acc = jnp.dot(h_scr[...], w_ref[:, part * tn + c0:part * tn + c0 + cw], preferred_element_type=F32)
            for hh in range(cw // HEAD_DIM):
                sl = slice(c0 + hh * HEAD_DIM, c0 + (hh + 1) * HEAD_DIM)
                a = acc[:, hh * HEAD_DIM:(hh + 1) * HEAD_DIM]
                if part == 0:
                    a = _rms(a, gq_ref[...])
                elif part == 1:
                    a = _rms(a, gk_ref[...])
                elif part == 3:
                    a = jax.nn.gelu(a)
                elif part == 4:
                    a = _rms(jax.nn.gelu(a), gv_ref[:, sl])
                out_ref[:, sl] = a.astype(out_ref.dtype)


def _inproj(x, mod, g_norm, w_in_bf, g_q, g_k, g_v, *, tm, tiles_per_batch, q_dtype, vg_dtype):
    rows, d = x.shape
    n_tot = w_in_bf.shape[1]
    tn = n_tot // 5
    n_heads = tn // HEAD_DIM
    r = mod.shape[1]
    modspec = lambda m: pl.BlockSpec((None, r, d), lambda i, m=m: (i // tiles_per_batch, 0, m))
    rowspec = pl.BlockSpec((tm, tn), lambda i: (i, 0))
    vec = lambda n: pl.BlockSpec((1, n), lambda i: (0, 0))
    outs = pl.pallas_call(
        functools.partial(_inproj_kernel, n_heads=n_heads),
        grid=(rows // tm,),
        in_specs=[pl.BlockSpec((tm, d), lambda i: (i, 0)),
                  modspec(1), modspec(0), vec(d),
                  pl.BlockSpec((d, n_tot), lambda i: (0, 0), pipeline_mode=pl.Buffered(1)),
                  vec(HEAD_DIM), vec(HEAD_DIM), vec(tn)],
        out_specs=[rowspec] * 5,
        out_shape=[jax.ShapeDtypeStruct((rows, tn), dt) for dt in (q_dtype, F32, F32, F32, vg_dtype)],
        scratch_shapes=[pltpu.VMEM((tm, d), BF16)],
        compiler_params=_cparams(("arbitrary",)),
        name="in_proj",
    )(x, mod, mod, g_norm.reshape(1, d), w_in_bf, g_q.reshape(1, -1), g_k.reshape(1, -1), g_v.reshape(1, -1))
    return outs


def _moba_prompt_kernel(slopes_ref, q_ref, k_ref, v_ref, o_ref, kb_scr, vb_scr, km_scr, al_scr,
                        *, n_blocks, heads_per_step):
    hg = pl.program_id(1)
    qi = pl.program_id(2)
    blk = MOBA_BLOCK
    tile = 2 * blk
    c1 = HEAD_DIM ** -0.5 * LOG2E
    heads = range(heads_per_step)
    nslope = [-slopes_ref[hg * heads_per_step + e] * LOG2E for e in heads]

    @pl.when(qi == 0)
    def _():
        km_scr[...] = jnp.zeros(km_scr.shape, F32)
        lane = lax.broadcasted_iota(jnp.int32, (blk, LANES), 1)
        rel = (lax.broadcasted_iota(jnp.int32, (tile, tile), 0)
               - lax.broadcasted_iota(jnp.int32, (tile, tile), 1)).astype(F32)
        for e in heads:
            cols = slice(e * HEAD_DIM, (e + 1) * HEAD_DIM)
            for n in range(n_blocks):
                rows = slice(n * blk, (n + 1) * blk)
                kblk = k_ref[rows, cols]
                kb_scr[e, rows, 0:HEAD_DIM] = kblk.astype(BF16)
                kb_scr[e, rows, HEAD_DIM:] = jnp.where(lane == n, 1.0, 0.0).astype(BF16)
                vb_scr[e, rows, :] = v_ref[rows, cols].astype(BF16)
                km_scr[e, n:n + 1, :] = jnp.mean(kblk, axis=0, keepdims=True)
            al_scr[e] = nslope[e] * rel

    def augmented_query(e):
        q = q_ref[:, e * HEAD_DIM:(e + 1) * HEAD_DIM]
        g = lax.dot_general(km_scr[e].astype(BF16), q, _NT, preferred_element_type=F32)[:n_blocks]
        bidx = lax.broadcasted_iota(jnp.int32, g.shape, 0)
        own = 2 * qi + lax.broadcasted_iota(jnp.int32, g.shape, 1) // blk
        sel = _topk_select(g, bidx < own, n_blocks, axis=0)
        bias = jnp.where(sel | (bidx == own), 0.0, NEG).astype(F32)
        bias = jnp.concatenate([bias, jnp.full((LANES - n_blocks, tile), NEG, F32)], axis=0)
        return jnp.concatenate([q, bias.T.astype(BF16)], axis=1)

    q_aug = [augmented_query(e) for e in heads]

    def tile_scores(e, t):
        start = pl.multiple_of(t * tile, tile)
        raw = lax.dot_general(q_aug[e], kb_scr[e, pl.ds(start, tile), :], _NT, preferred_element_type=F32)
        off = nslope[e] * ((qi - t) * tile).astype(F32)
        return raw * c1 + al_scr[e], off, start

    causal = (lax.broadcasted_iota(jnp.int32, (tile, tile), 1)
              <= lax.broadcasted_iota(jnp.int32, (tile, tile), 0))
    carry = []
    for e in heads:
        s, off, start = tile_scores(e, qi)
        s = jnp.where(causal, s, NEG)
        m = jnp.max(s, axis=1, keepdims=True) + off
        p = jnp.exp2(s - (m - off))
        l = jnp.sum(p, axis=1, keepdims=True)
        acc = jnp.dot(p.astype(BF16), vb_scr[e, pl.ds(start, tile), :], preferred_element_type=F32)
        carry += [m, l, acc]

    def body(t, carry):
        out = []
        for e in heads:
            m, l, acc = carry[3 * e:3 * e + 3]
            s, off, start = tile_scores(e, t)
            m_new = jnp.maximum(m, jnp.max(s, axis=1, keepdims=True) + off)
            alpha = jnp.exp2(m - m_new)
            p = jnp.exp2(s - (m_new - off))
            l = alpha * l + jnp.sum(p, axis=1, keepdims=True)
            acc = alpha * acc + jnp.dot(p.astype(BF16), vb_scr[e, pl.ds(start, tile), :],
                                        preferred_element_type=F32)
            out += [m_new, l, acc]
        return tuple(out)

    carry = lax.fori_loop(0, qi, body, tuple(carry))
    for e in heads:
        _, l, acc = carry[3 * e:3 * e + 3]
        o_ref[:, e * HEAD_DIM:(e + 1) * HEAD_DIM] = (acc / l).astype(o_ref.dtype)


def _moba_prompt(q, k, v, slopes, heads_per_step=2):
    b, s, da = k.shape
    n_heads = da // HEAD_DIM
    n_blocks = s // MOBA_BLOCK
    hw = heads_per_step * HEAD_DIM
    assert s % (2 * MOBA_BLOCK) == 0 and n_blocks >= MOBA_TOPK and n_blocks <= LANES
    assert n_heads % heads_per_step == 0
    tile = 2 * MOBA_BLOCK
    qspec = pl.BlockSpec((None, tile, hw), lambda bb, hh, qi: (bb, qi, hh))
    kvspec = pl.BlockSpec((None, s, hw), lambda bb, hh, qi: (bb, 0, hh))
    return pl.pallas_call(
        functools.partial(_moba_prompt_kernel, n_blocks=n_blocks, heads_per_step=heads_per_step),
        grid=(b, n_heads // heads_per_step, s // tile),
        in_specs=[pl.BlockSpec(memory_space=pltpu.SMEM), qspec, kvspec, kvspec],
        out_specs=qspec,
        out_shape=jax.ShapeDtypeStruct((b, s, da), F32),
        scratch_shapes=[pltpu.VMEM((heads_per_step, s, 2 * HEAD_DIM), BF16),
                        pltpu.VMEM((heads_per_step, s, HEAD_DIM), BF16),
                        pltpu.VMEM((heads_per_step, LANES, HEAD_DIM), F32),
                        pltpu.VMEM((heads_per_step, tile, tile), F32)],
        compiler_params=_cparams(("arbitrary", "arbitrary", "arbitrary")),
        name="moba_prompt",
    )(slopes, q, k, v)


def _moba_sample_kernel(pt_ref, q_ref, kn_ref, vn_ref, slope_ref, *refs, n_heads, n_q, past_len, n_blocks):
    del pt_ref
    n_pages = 2 * n_blocks
    k_refs, v_refs, o_ref = refs[:n_pages], refs[n_pages:2 * n_pages], refs[2 * n_pages]
    nrow = n_q * n_heads
    scale = HEAD_DIM ** -0.5
    page = k_refs[0].shape[0]
    pkeys = page * n_heads
    lane128 = lax.broadcasted_iota(jnp.int32, (nrow, LANES), 1)
    rowc = lax.broadcasted_iota(jnp.int32, (nrow, 1), 0)
    tq = past_len + rowc // n_heads
    slope = slope_ref[...]
    qb = q_ref[...].astype(BF16)
    qf = qb.astype(F32)

    def masked_scores(keys_bf, pos0, width, extra_ok=None):
        lane = lax.broadcasted_iota(jnp.int32, (nrow, width), 1)
        kpos = pos0 + lane // n_heads
        s = lax.dot_general(qb, keys_bf, _NT, preferred_element_type=F32) * scale
        s = s - slope * (tq - kpos).astype(F32)
        ok = (lane % n_heads) == (rowc % n_heads)
        if extra_ok is not None:
            ok = ok & extra_ok(lane, kpos)
        return jnp.where(ok, s, NEG)

    g_all = jnp.zeros((nrow, LANES), F32)
    m_all = jnp.zeros((nrow, LANES), F32)
    l_all = jnp.zeros((nrow, LANES), F32)
    outs = []
    for n in range(n_blocks):
        k0 = k_refs[2 * n][...]
        k1 = k_refs[2 * n + 1][...]
        s0 = masked_scores(k0.reshape(pkeys, HEAD_DIM).astype(BF16), n * MOBA_BLOCK, pkeys)
        s1 = masked_scores(k1.reshape(pkeys, HEAD_DIM).astype(BF16), n * MOBA_BLOCK + page, pkeys)
        m_n = jnp.maximum(jnp.max(s0, axis=1, keepdims=True), jnp.max(s1, axis=1, keepdims=True))
        p0 = jnp.exp(s0 - m_n)
        p1 = jnp.exp(s1 - m_n)
        l_n = jnp.sum(p0, axis=1, keepdims=True) + jnp.sum(p1, axis=1, keepdims=True)
        outs.append(jnp.dot(p0.astype(BF16), v_refs[2 * n][...].reshape(pkeys, HEAD_DIM).astype(BF16),
                            preferred_element_type=F32)
                    + jnp.dot(p1.astype(BF16), v_refs[2 * n + 1][...].reshape(pkeys, HEAD_DIM).astype(BF16),
                              preferred_element_type=F32))
        kmean = ((jnp.sum(k0, axis=0) + jnp.sum(k1, axis=0)) / MOBA_BLOCK).astype(BF16).astype(F32)
        g_n = jnp.sum(qf * jnp.concatenate([kmean] * n_q, axis=0), axis=1, keepdims=True)
        here = lane128 == n
        g_all = jnp.where(here, g_n, g_all)
        m_all = jnp.where(here, m_n, m_all)
        l_all = jnp.where(here, l_n, l_all)

    sel = _topk_select(g_all, lane128 < n_blocks, n_blocks, axis=1)
    pad = jnp.zeros((LANES - nrow, HEAD_DIM), F32)
    own_k = jnp.concatenate([kn_ref[...], pad], axis=0).astype(BF16)
    own_v = jnp.concatenate([vn_ref[...], pad], axis=0).astype(BF16)
    s_own = masked_scores(own_k, past_len, LANES, lambda lane, kpos: (lane < nrow) & (kpos <= tq))
    m_fin = jnp.maximum(jnp.max(jnp.where(sel, m_all, NEG), axis=1, keepdims=True),
                        jnp.max(s_own, axis=1, keepdims=True))
    w = jnp.where(sel, jnp.exp(jnp.where(sel, m_all - m_fin, 0.0)), 0.0)
    p_own = jnp.exp(s_own - m_fin)
    l_fin = jnp.sum(w * l_all, axis=1, keepdims=True) + jnp.sum(p_own, axis=1, keepdims=True)
    comb = jnp.dot(p_own.astype(BF16), own_v, preferred_element_type=F32)
    for n in range(n_blocks):
        comb = comb + w[:, n:n + 1] * outs[n]
    o_ref[...] = (comb / l_fin).astype(o_ref.dtype)


def _moba_sample(q, k_new, v_new, cache_k, cache_v, layer, page_table, slopes_np, past_len):
    db, nrow, hd = q.shape
    page, n_heads = cache_k.shape[2:4]
    n_q = nrow // n_heads
    n_pages = page_table.shape[1]
    n_blocks = past_len // MOBA_BLOCK
    assert past_len % MOBA_BLOCK == 0 and MOBA_BLOCK == 2 * page and n_pages * page == past_len
    assert n_blocks >= MOBA_TOPK and n_blocks <= LANES and n_heads == SUBLANES
    assert nrow % 16 == 0 and nrow <= LANES
    slope_col = np.tile(slopes_np, n_q).reshape(nrow, 1)
    seqspec = pl.BlockSpec((None, nrow, hd), lambda s, pt: (s, 0, 0))
    pagespecs = [pl.BlockSpec((None, None, page, n_heads, hd),
                              lambda s, pt, j=j: (layer, pt[s * n_pages + j], 0, 0, 0))
                 for j in range(n_pages)]
    grid_spec = pltpu.PrefetchScalarGridSpec(
        num_scalar_prefetch=1,
        grid=(db,),
        in_specs=[seqspec, seqspec, seqspec, pl.BlockSpec((nrow, 1), lambda s, pt: (0, 0))] + pagespecs * 2,
        out_specs=seqspec,
    )
    return pl.pallas_call(
        functools.partial(_moba_sample_kernel, n_heads=n_heads, n_q=n_q, past_len=past_len, n_blocks=n_blocks),
        grid_spec=grid_spec,
        out_shape=jax.ShapeDtypeStruct((db, nrow, hd), F32),
        compiler_params=_cparams(("arbitrary",)),
        name="moba_sample",
    )(page_table.reshape(-1), q, k_new, v_new, jnp.asarray(slope_col),
      *([cache_k] * n_pages), *([cache_v] * n_pages))


def _outproj_kernel(oa_ref, u_ref, vg_ref, x_ref, gt_ref, sc_ref, sh_ref, ga_ref, gg_ref, gf_ref,
                    wo_ref, ws_ref, bs_ref, x1_ref, h2_ref, og_scr, *, time_major, n_groups):
    tm = x_ref.shape[0]
    da = oa_ref.shape[1]
    if time_major:
        n_t = ws_ref.shape[0]
        slab = tm // n_t
        for t in range(n_t):
            mixed = jnp.zeros((slab, da), F32) + bs_ref[t:t + 1, :]
            for s in range(t + 1):
                mixed = mixed + ws_ref[t, s:s + 1, :] * vg_ref[s * slab:(s + 1) * slab, :].astype(F32)
            og_scr[t * slab:(t + 1) * slab, :] = u_ref[t * slab:(t + 1) * slab, :] * mixed
    else:
        tri = (lax.broadcasted_iota(jnp.int32, (CHUNK, CHUNK), 0)
               >= lax.broadcasted_iota(jnp.int32, (CHUNK, CHUNK), 1))
        for gi in range(n_groups):
            sl = slice(gi * GROUP_DIM, (gi + 1) * GROUP_DIM)
            w = jnp.where(tri, ws_ref[gi], 0.0).astype(BF16)
            for c in range(tm // CHUNK):
                rs = slice(c * CHUNK, (c + 1) * CHUNK)
                mixed = jnp.dot(w, vg_ref[rs, sl].astype(BF16), preferred_element_type=F32) + bs_ref[:, sl]
                og_scr[rs, sl] = u_ref[rs, sl] * mixed
    pm = min(tm, 256)
    gt, sc, sh = (_tile_rows(m_ref[...], pm) for m_ref in (gt_ref, sc_ref, sh_ref))
    for r in range(tm // pm):
        rows = slice(r * pm, (r + 1) * pm)
        a_n = _rms(oa_ref[rows, :], ga_ref[...]).astype(BF16)
        g_n = _rms(og_scr[rows, :], gg_ref[...]).astype(BF16)
        out = (jnp.dot(a_n, wo_ref[0:da, :], preferred_element_type=F32)
               + jnp.dot(g_n, wo_ref[da:, :], preferred_element_type=F32))
        x1 = x_ref[rows, :] + gt * out
        x1_ref[rows, :] = x1
        h2 = _rms(x1, gf_ref[...]) * (1.0 + sc) + sh
        h2_ref[rows, :] = h2.astype(h2_ref.dtype)


def _outproj(oa, u, vg, x, mod, g_oa, g_og, g_ffn, w_out_bf, ws, bs, *, tm, tiles_per_batch, time_major):
    rows, d = x.shape
    da = oa.shape[1]
    r = mod.shape[1]
    modspec = lambda m: pl.BlockSpec((None, r, d), lambda i, m=m: (i // tiles_per_batch, 0, m))
    row = lambda n: pl.BlockSpec((tm, n), lambda i: (i, 0))
    full = lambda a: pl.BlockSpec(a.shape, lambda i, nd=a.ndim: (0,) * nd, pipeline_mode=pl.Buffered(1))
    g_oa, g_og, g_ffn = g_oa.reshape(1, -1), g_og.reshape(1, -1), g_ffn.reshape(1, -1)
    return pl.pallas_call(
        functools.partial(_outproj_kernel, time_major=time_major, n_groups=da // GROUP_DIM),
        grid=(rows // tm,),
        in_specs=[row(da), row(da), row(da), row(d), modspec(2), modspec(4), modspec(3),
                  full(g_oa), full(g_og), full(g_ffn), full(w_out_bf), full(ws), full(bs)],
        out_specs=[row(d), row(d)],
        out_shape=[jax.ShapeDtypeStruct((rows, d), F32), jax.ShapeDtypeStruct((rows, d), BF16)],
        scratch_shapes=[pltpu.VMEM((tm, da), F32)],
        compiler_params=_cparams(("arbitrary",)),
        name="gmlp_out_proj",
    )(oa, u, vg, x, mod, mod, mod, g_oa, g_og, g_ffn, w_out_bf, ws, bs)


def _ffn_kernel(h_ref, x1_ref, gt_ref, wa_ref, wp_ref, wc_ref, bc_ref, wd_ref, prev_ref,
                y_ref, conv_ref, abuf, halo, acc, *, shift, tiles_per_batch, n_parts):
    i = pl.program_id(0)
    f = pl.program_id(1)
    nf = pl.num_programs(1)
    tm = h_ref.shape[0]
    keep = (CONV_W - 1) * shift
    off = abuf.shape[0] - tm

    @pl.when(i % tiles_per_batch == 0)
    def _():
        abuf[off - keep:off, :] = prev_ref[...]

    @pl.when(i % tiles_per_batch != 0)
    def _():
        abuf[off - keep:off, :] = halo[f]

    @pl.when(f == 0)
    def _():
        acc[...] = jnp.zeros(acc.shape, F32)

    hm = tm // n_parts
    a_parts, p_parts = [], []
    for r in range(n_parts):
        rows = slice(r * hm, (r + 1) * hm)
        h = h_ref[rows, :]
        a = jnp.dot(h, wa_ref[...], preferred_element_type=F32)
        abuf[off + r * hm:off + (r + 1) * hm, :] = a
        a_parts.append(a)
        p_parts.append(jnp.dot(h, wp_ref[...], preferred_element_type=F32))
    for r in range(n_parts):
        rows = slice(r * hm, (r + 1) * hm)
        ac = bc_ref[...] + wc_ref[CONV_W - 1:CONV_W, :] * a_parts[r]
        for jj in range(CONV_W - 1):
            start = off + r * hm - (CONV_W - 1 - jj) * shift
            ac = ac + wc_ref[jj:jj + 1, :] * abuf[start:start + hm, :]
        gate = (ac * jax.nn.sigmoid(ac)) * p_parts[r]
        acc[rows, :] += jnp.dot(gate.astype(BF16), wd_ref[...], preferred_element_type=F32)
    last = abuf[off + tm - keep:off + tm, :]
    halo[f] = last
    conv_ref[...] = last

    @pl.when(f == nf - 1)
    def _():
        y_ref[...] = x1_ref[...] + _tile_rows(gt_ref[...], tm) * acc[...]


def _ffn(h2, x1, mod, w_up_bf, w_conv, b_conv, w_down_bf, conv_prev, *, tm, tf, tiles_per_batch, shift):
    rows, d = x1.shape
    ff = w_down_bf.shape[0]
    nf = ff // tf
    keep = conv_prev.shape[1]
    r = mod.shape[1]
    off = -(-keep // SUBLANES) * SUBLANES
    n_tiles = rows // tm
    y, tail = pl.pallas_call(
        functools.partial(_ffn_kernel, shift=shift, tiles_per_batch=tiles_per_batch, n_parts=2),
        grid=(n_tiles, nf),
        in_specs=[pl.BlockSpec((tm, d), lambda i, f: (i, 0)),
                  pl.BlockSpec((tm, d), lambda i, f: (i, 0)),
                  pl.BlockSpec((None, r, d), lambda i, f: (i // tiles_per_batch, 0, 5)),
                  pl.BlockSpec((d, tf), lambda i, f: (0, f)),
                  pl.BlockSpec((d, tf), lambda i, f: (0, f + nf)),
                  pl.BlockSpec((CONV_W, tf), lambda i, f: (0, f)),
                  pl.BlockSpec((1, tf), lambda i, f: (0, f)),
                  pl.BlockSpec((tf, d), lambda i, f: (f, 0)),
                  pl.BlockSpec((None, keep, tf), lambda i, f: (i // tiles_per_batch, 0, f))],
        out_specs=[pl.BlockSpec((tm, d), lambda i, f: (i, 0)),
                   pl.BlockSpec((None, keep, tf), lambda i, f: (i, 0, f))],
        out_shape=[jax.ShapeDtypeStruct((rows, d), F32), jax.ShapeDtypeStruct((n_tiles, keep, ff), F32)],
        scratch_shapes=[pltpu.VMEM((off + tm, tf), F32), pltpu.VMEM((nf, keep, tf), F32),
                        pltpu.VMEM((tm, d), F32)],
        compiler_params=_cparams(("arbitrary", "arbitrary")),
        name="conv_ffn",
    )(h2, x1, mod, w_up_bf, w_up_bf, w_conv, b_conv.reshape(1, ff), w_down_bf, conv_prev)
    return y, tail[tiles_per_batch - 1::tiles_per_batch]


def kernel(x_prompt, x_sample, cache_k, cache_v, state_conv, page_table, c_prompt, c_sample, w_ada, b_ada, g_norm_mix, w_in, g_q, g_k, g_v, w_s, b_s, g_out_attn, g_out_gmlp, w_out, g_norm_ffn, w_up, w_conv, b_conv, w_down):
    depth = w_ada.shape[0]
    b, s, d = x_prompt.shape
    db, n_q, _ = x_sample.shape
    page, n_heads, hd = cache_k.shape[2:]
    da = n_heads * hd
    ff = w_down.shape[1]
    n_groups = w_s.shape[1]
    past_len = page_table.shape[1] * page
    assert hd == HEAD_DIM and da == n_groups * GROUP_DIM and s % CHUNK == 0 and n_q <= CHUNK
    assert w_in.shape[2] == 5 * da
    slopes_np = (2.0 ** (-8.0 * np.arange(1, n_heads + 1) / n_heads)).astype(np.float32)
    slopes = jnp.asarray(slopes_np)

    n_c = b + db
    pad_c = -(-n_c // SUBLANES) * SUBLANES
    c_all = jnp.concatenate([c_prompt, c_sample, jnp.zeros((pad_c - n_c, d), F32)], axis=0)

    h_p = x_prompt.reshape(b * s, d)
    h_s = x_sample.transpose(1, 0, 2).reshape(n_q * db, d)
    tm_p = 512
    outs = [[] for _ in range(7)]
    for layer in range(depth):
        w_in_bf = w_in[layer].astype(BF16)
        w_out_bf = w_out[layer].astype(BF16)
        w_up_bf = w_up[layer].astype(BF16)
        w_down_bf = w_down[layer].astype(BF16)

        mod = _ada(c_all, w_ada[layer], b_ada[layer])
        mod_p = mod[:b].reshape(b, 1, N_MOD * d)
        mod_s = mod[b:n_c].reshape(1, db, N_MOD * d)

        q, k, v, u, vg = _inproj(h_p, mod_p, g_norm_mix[layer], w_in_bf, g_q[layer], g_k[layer], g_v[layer],
                                 tm=tm_p, tiles_per_batch=s // tm_p, q_dtype=BF16, vg_dtype=BF16)
        o_attn = _moba_prompt(q.reshape(b, s, da), k.reshape(b, s, da), v.reshape(b, s, da), slopes)
        bs_rows = jnp.repeat(b_s[layer].T, GROUP_DIM, axis=1)
        x1, h2 = _outproj(o_attn.reshape(b * s, da), u, vg, h_p, mod_p, g_out_attn[layer], g_out_gmlp[layer],
                          g_norm_ffn[layer], w_out_bf, w_s[layer], bs_rows,
                          tm=tm_p, tiles_per_batch=s // tm_p, time_major=False)
        conv0 = jnp.zeros((b, CONV_W - 1, ff), F32)
        y_p, conv_p = _ffn(h2, x1, mod_p, w_up_bf, w_conv[layer], b_conv[layer], w_down_bf, conv0,
                           tm=tm_p, tf=512, tiles_per_batch=s // tm_p, shift=1)

        qs, ks, vs, us, vgs = _inproj(h_s, mod_s, g_norm_mix[layer], w_in_bf, g_q[layer], g_k[layer], g_v[layer],
                                      tm=db, tiles_per_batch=n_q, q_dtype=F32, vg_dtype=F32)
        seq_major = lambda a: a.reshape(n_q, db, -1).transpose(1, 0, 2)
        per_head = lambda a: seq_major(a).reshape(db, n_q * n_heads, hd)
        ks_h, vs_h = per_head(ks), per_head(vs)
        o_s = _moba_sample(per_head(qs), ks_h, vs_h, cache_k, cache_v, layer, page_table, slopes_np, past_len)
        o_s = o_s.reshape(db, n_q, da).transpose(1, 0, 2).reshape(n_q * db, da)
        ws_t = jnp.repeat(jnp.tril(w_s[layer][:, :n_q, :n_q]).transpose(1, 2, 0), GROUP_DIM, axis=2)
        bs_t = jnp.repeat(b_s[layer][:, :n_q].T, GROUP_DIM, axis=1)
        x1s, h2s = _outproj(o_s, us, vgs, h_s, mod_s, g_out_attn[layer], g_out_gmlp[layer], g_norm_ffn[layer],
                            w_out_bf, ws_t, bs_t, tm=n_q * db, tiles_per_batch=1, time_major=True)
        prev_s = state_conv[layer].transpose(1, 0, 2).reshape(1, (CONV_W - 1) * db, ff)
        y_s, conv_s = _ffn(h2s, x1s, mod_s, w_up_bf, w_conv[layer], b_conv[layer], w_down_bf, prev_s,
                           tm=n_q * db, tf=512, tiles_per_batch=1, shift=db)

        h_p, h_s = y_p, y_s
        outs[0].append(k.reshape(b, s // page, page, n_heads, hd))
        outs[1].append(v.reshape(b, s // page, page, n_heads, hd))
        outs[2].append(ks_h.reshape(db, n_q, n_heads, hd))
        outs[3].append(vs_h.reshape(db, n_q, n_heads, hd))
        outs[4].append(conv_p)
        outs[5].append(conv_s.reshape(CONV_W - 1, db, ff).transpose(1, 0, 2))
        outs[6].append(seq_major(vgs))

    y_prompt = h_p.reshape(b, s, d)
    y_sample = h_s.reshape(n_q, db, d).transpose(1, 0, 2)
    return (y_prompt, y_sample) + tuple(jnp.stack(o) for o in outs)
```

```python
import functools
import math

import numpy as np
import jax
import jax.numpy as jnp
from jax import lax
from jax.experimental import pallas as pl
from jax.experimental.pallas import tpu as pltpu

F32 = jnp.float32
BF16 = jnp.bfloat16

HEAD_DIM = 128
GROUP_DIM = 128
MOBA_BLOCK = 256
MOBA_TOPK = 3
CHUNK = 128
CONV_W = 3
N_MOD = 6
EPS = 1e-6
NEG = -1e30
LOG2E = math.log2(math.e)
LANES = 128
SUBLANES = 8
SCORE_ROWS = 64
VMEM_LIMIT = 56 * 1024 * 1024

_NT = (((1,), (1,)), ((), ()))


def _cparams(sem):
    return pltpu.CompilerParams(dimension_semantics=sem, vmem_limit_bytes=VMEM_LIMIT)


def _rms(x, g):
    ms = jnp.mean(x * x, axis=-1, keepdims=True)
    return (x * lax.rsqrt(ms + EPS)) * g


def _tile_rows(m, rows):
    r = m.shape[0]
    if r == 1 or r == rows:
        return m
    return jnp.concatenate([m] * (rows // r), axis=0)


def _topk_select(g, valid, n_blocks, axis):
    blk = lax.broadcasted_iota(jnp.int32, g.shape, axis)
    gm = jnp.where(valid, g, -jnp.inf)
    rank = jnp.zeros(g.shape, jnp.int32)
    for m in range(n_blocks):
        one = gm[m:m + 1, :] if axis == 0 else gm[:, m:m + 1]
        beats = (one > gm) | ((one == gm) & (m < blk))
        rank = rank + beats.astype(jnp.int32)
    return valid & (rank < MOBA_TOPK)


def _ada_kernel(c_ref, w_ref, b_ref, o_ref):
    c = c_ref[...]
    s = (c * jax.nn.sigmoid(c)).astype(BF16)
    o_ref[...] = jnp.dot(s, w_ref[...].astype(BF16), preferred_element_type=F32) + b_ref[...]


def _ada(c, w, b):
    m, d = c.shape
    n = w.shape[1]
    tn = 2048
    return pl.pallas_call(
        _ada_kernel,
        grid=(n // tn,),
        in_specs=[pl.BlockSpec((m, d), lambda j: (0, 0)),
                  pl.BlockSpec((d, tn), lambda j: (0, j)),
                  pl.BlockSpec((1, tn), lambda j: (0, j))],
        out_specs=pl.BlockSpec((m, tn), lambda j: (0, j)),
        out_shape=jax.ShapeDtypeStruct((m, n), F32),
        compiler_params=_cparams(("arbitrary",)),
        name="ada_mod",
    )(c, w, b.reshape(1, n))


def _inproj_kernel(x_ref, sc_ref, sh_ref, gn_ref, w_ref, gq_ref, gk_ref, gv_ref,
                   q_ref, k_ref, v_ref, u_ref, vg_ref, h_scr, *, n_heads):
    tm = x_ref.shape[0]
    y = _rms(x_ref[...], gn_ref[...])
    h = y * (1.0 + _tile_rows(sc_ref[...], tm)) + _tile_rows(sh_ref[...], tm)
    h_scr[...] = h.astype(BF16)

    tn = n_heads * HEAD_DIM
    cw = 2 * HEAD_DIM
    for part, out_ref in enumerate((q_ref, k_ref, v_ref, u_ref, vg_ref)):
        for c0 in range(0, tn, cw):
            acc = jnp.dot(h_scr[...], w_ref[:, part * tn + c0:part * tn + c0 + cw], preferred_element_type=F32)
            for hh in range(cw // HEAD_DIM):
                sl = slice(c0 + hh * HEAD_DIM, c0 + (hh + 1) * HEAD_DIM)
                a = acc[:, hh * HEAD_DIM:(hh + 1) * HEAD_DIM]
                if part == 0:
                    a = _rms(a, gq_ref[...])
                elif part == 1:
                    a = _rms(a, gk_ref[...])
                elif part == 3:
                    a = jax.nn.gelu(a)
                elif part == 4:
                    a = _rms(jax.nn.gelu(a), gv_ref[:, sl])
                out_ref[:, sl] = a.astype(out_ref.dtype)


def _inproj(x, mod, g_norm, w_in_bf, g_q, g_k, g_v, *, tm, tiles_per_batch, q_dtype, vg_dtype):
    rows, d = x.shape
    n_tot = w_in_bf.shape[1]
    tn = n_tot // 5
    n_heads = tn // HEAD_DIM
    r = mod.shape[1]
    modspec = lambda m: pl.BlockSpec((None, r, d), lambda i, m=m: (i // tiles_per_batch, 0, m))
    rowspec = pl.BlockSpec((tm, tn), lambda i: (i, 0))
    vec = lambda n: pl.BlockSpec((1, n), lambda i: (0, 0))
    outs = pl.pallas_call(
        functools.partial(_inproj_kernel, n_heads=n_heads),
        grid=(rows // tm,),
        in_specs=[pl.BlockSpec((tm, d), lambda i: (i, 0)),
                  modspec(1), modspec(0), vec(d),
                  pl.BlockSpec((d, n_tot), lambda i: (0, 0), pipeline_mode=pl.Buffered(1)),
                  vec(HEAD_DIM), vec(HEAD_DIM), vec(tn)],
        out_specs=[rowspec] * 5,
        out_shape=[jax.ShapeDtypeStruct((rows, tn), dt) for dt in (q_dtype, F32, F32, F32, vg_dtype)],
        scratch_shapes=[pltpu.VMEM((tm, d), BF16)],
        compiler_params=_cparams(("arbitrary",)),
        name="in_proj",
    )(x, mod, mod, g_norm.reshape(1, d), w_in_bf, g_q.reshape(1, -1), g_k.reshape(1, -1), g_v.reshape(1, -1))
    return outs


def _moba_prompt_kernel(slopes_ref, q_ref, k_ref, v_ref, o_ref, kb_scr, vb_scr, km_scr, al_scr,
                        qa_scr, raw_scr, p_scr, m_scr, l_scr, a_scr, acc_scr, *, n_blocks, heads_per_step):
    hg = pl.program_id(1)
    qi = pl.program_id(2)
    blk = MOBA_BLOCK
    tile = 2 * blk
    c1 = HEAD_DIM ** -0.5 * LOG2E
    heads = range(heads_per_step)
    nslope = [-slopes_ref[hg * heads_per_step + e] * LOG2E for e in heads]

    @pl.when(qi == 0)
    def _():
        km_scr[...] = jnp.zeros(km_scr.shape, F32)
        lane = lax.broadcasted_iota(jnp.int32, (blk, LANES), 1)
        rel = (lax.broadcasted_iota(jnp.int32, (tile, tile), 0)
               - lax.broadcasted_iota(jnp.int32, (tile, tile), 1)).astype(F32)
        for e in heads:
            cols = slice(e * HEAD_DIM, (e + 1) * HEAD_DIM)
            for n in range(n_blocks):
                rows = slice(n * blk, (n + 1) * blk)
                kblk = k_ref[rows, cols]
                kb_scr[e, rows, 0:HEAD_DIM] = kblk.astype(BF16)
                kb_scr[e, rows, HEAD_DIM:] = jnp.where(lane == n, 1.0, 0.0).astype(BF16)
                vb_scr[e, rows, :] = v_ref[rows, cols].astype(BF16)
                km_scr[e, n:n + 1, :] = jnp.mean(kblk, axis=0, keepdims=True)
            al_scr[e] = nslope[e] * rel

    def augmented_query(e):
        q = q_ref[:, e * HEAD_DIM:(e + 1) * HEAD_DIM]
        g = lax.dot_general(km_scr[e].astype(BF16), q, _NT, preferred_element_type=F32)[:n_blocks]
        bidx = lax.broadcasted_iota(jnp.int32, g.shape, 0)
        own = 2 * qi + lax.broadcasted_iota(jnp.int32, g.shape, 1) // blk
        sel = _topk_select(g, bidx < own, n_blocks, axis=0)
        bias = jnp.where(sel | (bidx == own), 0.0, NEG).astype(F32)
        bias = jnp.concatenate([bias, jnp.full((LANES - n_blocks, tile), NEG, F32)], axis=0)
        return jnp.concatenate([q, bias.T.astype(BF16)], axis=1)

    for e in heads:
        qa_scr[e] = augmented_query(e)

    rc = SCORE_ROWS
    coli = lax.broadcasted_iota(jnp.int32, (rc, tile), 1)
    rowi = lax.broadcasted_iota(jnp.int32, (rc, tile), 0)

    def score(e, t):
        start = pl.multiple_of(t * tile, tile)
        raw_scr[e] = lax.dot_general(qa_scr[e], kb_scr[e, pl.ds(start, tile), :], _NT,
                                     preferred_element_type=F32)

    def fold(e, t, diagonal):
        start = pl.multiple_of(t * tile, tile)
        off = nslope[e] * ((qi - t) * tile).astype(F32)
        for c in range(tile // rc):
            rows = slice(c * rc, (c + 1) * rc)
            s = raw_scr[e, rows, :] * c1 + al_scr[e, rows, :]
            if diagonal:
                s = jnp.where(coli <= rowi + c * rc, s, NEG)
                m_new = jnp.broadcast_to(jnp.max(s, axis=1, keepdims=True) + off, (rc, LANES))
            else:
                m_old = m_scr[e, rows, :]
                m_new = jnp.maximum(m_old, jnp.max(s, axis=1, keepdims=True) + off)
                alpha = jnp.exp2(m_old - m_new)
                a_scr[e, rows, :] = alpha
            p = jnp.exp2(s - jnp.concatenate([m_new - off] * (tile // LANES), axis=1))
            psum = jnp.sum(p, axis=1, keepdims=True)
            p_scr[e, rows, :] = p.astype(BF16)
            m_scr[e, rows, :] = m_new
            if diagonal:
                l_scr[e, rows, :] = jnp.broadcast_to(psum, (rc, LANES))
            else:
                l_scr[e, rows, :] = alpha * l_scr[e, rows, :] + psum
        pv = jnp.dot(p_scr[e], vb_scr[e, pl.ds(start, tile), :], preferred_element_type=F32)
        if diagonal:
            acc_scr[e] = pv
        else:
            acc_scr[e] = a_scr[e] * acc_scr[e] + pv

    def visit(t, diagonal):
        for e in heads:
            score(e, t)
        for e in heads:
            fold(e, t, diagonal)

    visit(qi, True)

    def body(t, carry):
        visit(t, False)
        return carry

    lax.fori_loop(0, qi, body, 0)
    for e in heads:
        o_ref[:, e * HEAD_DIM:(e + 1) * HEAD_DIM] = (acc_scr[e] / l_scr[e]).astype(o_ref.dtype)


def _moba_prompt(q, k, v, slopes, heads_per_step=2):
    b, s, da = k.shape
    n_heads = da // HEAD_DIM
    n_blocks = s // MOBA_BLOCK
    hw = heads_per_step * HEAD_DIM
    assert s % (2 * MOBA_BLOCK) == 0 and n_blocks >= MOBA_TOPK and n_blocks <= LANES
    assert n_heads % heads_per_step == 0
    tile = 2 * MOBA_BLOCK
    qspec = pl.BlockSpec((None, tile, hw), lambda bb, hh, qi: (bb, qi, hh))
    kvspec = pl.BlockSpec((None, s, hw), lambda bb, hh, qi: (bb, 0, hh))
    return pl.pallas_call(
        functools.partial(_moba_prompt_kernel, n_blocks=n_blocks, heads_per_step=heads_per_step),
        grid=(b, n_heads // heads_per_step, s // tile),
        in_specs=[pl.BlockSpec(memory_space=pltpu.SMEM), qspec, kvspec, kvspec],
        out_specs=qspec,
        out_shape=jax.ShapeDtypeStruct((b, s, da), F32),
        scratch_shapes=[pltpu.VMEM((heads_per_step, s, 2 * HEAD_DIM), BF16),
                        pltpu.VMEM((heads_per_step, s, HEAD_DIM), BF16),
                        pltpu.VMEM((heads_per_step, LANES, HEAD_DIM), F32),
                        pltpu.VMEM((heads_per_step, tile, tile), F32),
                        pltpu.VMEM((heads_per_step, tile, 2 * HEAD_DIM), BF16),
                        pltpu.VMEM((heads_per_step, tile, tile), F32),
                        pltpu.VMEM((heads_per_step, tile, tile), BF16),
                        pltpu.VMEM((heads_per_step, tile, LANES), F32),
                        pltpu.VMEM((heads_per_step, tile, LANES), F32),
                        pltpu.VMEM((heads_per_step, tile, LANES), F32),
                        pltpu.VMEM((heads_per_step, tile, HEAD_DIM), F32)],
        compiler_params=_cparams(("arbitrary", "arbitrary", "arbitrary")),
        name="moba_prompt",
    )(slopes, q, k, v)


def _moba_sample_kernel(pt_ref, q_ref, kn_ref, vn_ref, slope_ref, *refs, n_heads, n_q, past_len, n_blocks):
    del pt_ref
    n_pages = 2 * n_blocks
    k_refs, v_refs, o_ref = refs[:n_pages], refs[n_pages:2 * n_pages], refs[2 * n_pages]
    nrow = n_q * n_heads
    scale = HEAD_DIM ** -0.5
    page = k_refs[0].shape[0]
    pkeys = page * n_heads
    lane128 = lax.broadcasted_iota(jnp.int32, (nrow, LANES), 1)
    rowc = lax.broadcasted_iota(jnp.int32, (nrow, 1), 0)
    tq = past_len + rowc // n_heads
    slope = slope_ref[...]
    qb = q_ref[...].astype(BF16)
    qf = qb.astype(F32)

    def masked_scores(raw, pos0, width, extra_ok=None):
        lane = lax.broadcasted_iota(jnp.int32, (nrow, width), 1)
        kpos = pos0 + lane // n_heads
        s = raw * scale - slope * (tq - kpos).astype(F32)
        ok = (lane % n_heads) == (rowc % n_heads)
        if extra_ok is not None:
            ok = ok & extra_ok(lane, kpos)
        return jnp.where(ok, s, NEG)

    g_all = jnp.zeros((nrow, LANES), F32)
    m_all = jnp.zeros((nrow, LANES), F32)
    l_all = jnp.zeros((nrow, LANES), F32)
    zq = jnp.zeros_like(qb)
    q_pair = jnp.concatenate([jnp.concatenate([qb, zq], axis=1), jnp.concatenate([zq, qb], axis=1)], axis=0)
    as_keys = lambda ref: ref[...].reshape(pkeys, HEAD_DIM).astype(BF16)
    raws = []
    for n in range(n_blocks):
        k0 = k_refs[2 * n][...]
        k1 = k_refs[2 * n + 1][...]
        kk = jnp.concatenate([k0.reshape(pkeys, HEAD_DIM).astype(BF16),
                              k1.reshape(pkeys, HEAD_DIM).astype(BF16)], axis=1)
        raws.append(lax.dot_general(q_pair, kk, _NT, preferred_element_type=F32))
        kmean = ((jnp.sum(k0, axis=0) + jnp.sum(k1, axis=0)) / MOBA_BLOCK).astype(BF16).astype(F32)
        g_n = jnp.sum(qf * jnp.concatenate([kmean] * n_q, axis=0), axis=1, keepdims=True)
        g_all = jnp.where(lane128 == n, g_n, g_all)
    pps = []
    for n in range(n_blocks):
        s0 = masked_scores(raws[n][:nrow], n * MOBA_BLOCK, pkeys)
        s1 = masked_scores(raws[n][nrow:], n * MOBA_BLOCK + page, pkeys)
        m_n = jnp.maximum(jnp.max(s0, axis=1, keepdims=True), jnp.max(s1, axis=1, keepdims=True))
        p0 = jnp.exp(s0 - m_n)
        p1 = jnp.exp(s1 - m_n)
        l_n = jnp.sum(p0, axis=1, keepdims=True) + jnp.sum(p1, axis=1, keepdims=True)
        pps.append(jnp.concatenate([p0, p1], axis=0).astype(BF16))
        m_all = jnp.where(lane128 == n, m_n, m_all)
        l_all = jnp.where(lane128 == n, l_n, l_all)
    outs = []
    for n in range(n_blocks):
        vv = jnp.concatenate([as_keys(v_refs[2 * n]), as_keys(v_refs[2 * n + 1])], axis=1)
        o2 = jnp.dot(pps[n], vv, preferred_element_type=F32)
        outs.append(o2[:nrow, :HEAD_DIM] + o2[nrow:, HEAD_DIM:])

    sel = _topk_select(g_all, lane128 < n_blocks, n_blocks, axis=1)
    pad = jnp.zeros((LANES - nrow, HEAD_DIM), F32)
    own_k = jnp.concatenate([kn_ref[...], pad], axis=0).astype(BF16)
    own_v = jnp.concatenate([vn_ref[...], pad], axis=0).astype(BF16)
    s_own = masked_scores(lax.dot_general(qb, own_k, _NT, preferred_element_type=F32), past_len, LANES,
                          lambda lane, kpos: (lane < nrow) & (kpos <= tq))
    m_fin = jnp.maximum(jnp.max(jnp.where(sel, m_all, NEG), axis=1, keepdims=True),
                        jnp.max(s_own, axis=1, keepdims=True))
    w = jnp.where(sel, jnp.exp(jnp.where(sel, m_all - m_fin, 0.0)), 0.0)
    p_own = jnp.exp(s_own - m_fin)
    l_fin = jnp.sum(w * l_all, axis=1, keepdims=True) + jnp.sum(p_own, axis=1, keepdims=True)
    comb = jnp.dot(p_own.astype(BF16), own_v, preferred_element_type=F32)
    for n in range(n_blocks):
        comb = comb + w[:, n:n + 1] * outs[n]
    o_ref[...] = (comb / l_fin).astype(o_ref.dtype)


def _moba_sample(q, k_new, v_new, cache_k, cache_v, layer, page_table, slopes_np, past_len):
    db, nrow, hd = q.shape
    page, n_heads = cache_k.shape[2:4]
    n_q = nrow // n_heads
    n_pages = page_table.shape[1]
    n_blocks = past_len // MOBA_BLOCK
    assert past_len % MOBA_BLOCK == 0 and MOBA_BLOCK == 2 * page and n_pages * page == past_len
    assert n_blocks >= MOBA_TOPK and n_blocks <= LANES and n_heads == SUBLANES
    assert nrow % 16 == 0 and nrow <= LANES
    slope_col = np.tile(slopes_np, n_q).reshape(nrow, 1)
    seqspec = pl.BlockSpec((None, nrow, hd), lambda s, pt: (s, 0, 0))
    pagespecs = [pl.BlockSpec((None, None, page, n_heads, hd),
                              lambda s, pt, j=j: (layer, pt[s * n_pages + j], 0, 0, 0))
                 for j in range(n_pages)]
    grid_spec = pltpu.PrefetchScalarGridSpec(
        num_scalar_prefetch=1,
        grid=(db,),
        in_specs=[seqspec, seqspec, seqspec, pl.BlockSpec((nrow, 1), lambda s, pt: (0, 0))] + pagespecs * 2,
        out_specs=seqspec,
    )
    return pl.pallas_call(
        functools.partial(_moba_sample_kernel, n_heads=n_heads, n_q=n_q, past_len=past_len, n_blocks=n_blocks),
        grid_spec=grid_spec,
        out_shape=jax.ShapeDtypeStruct((db, nrow, hd), F32),
        compiler_params=_cparams(("arbitrary",)),
        name="moba_sample",
    )(page_table.reshape(-1), q, k_new, v_new, jnp.asarray(slope_col),
      *([cache_k] * n_pages), *([cache_v] * n_pages))


def _outproj_kernel(oa_ref, u_ref, vg_ref, x_ref, gt_ref, sc_ref, sh_ref, ga_ref, gg_ref, gf_ref,
                    wo_ref, ws_ref, bs_ref, x1_ref, h2_ref, og_scr, *, time_major, n_groups):
    tm = x_ref.shape[0]
    da = oa_ref.shape[1]
    if time_major:
        n_t = ws_ref.shape[0]
        slab = tm // n_t
        for t in range(n_t):
            mixed = jnp.zeros((slab, da), F32) + bs_ref[t:t + 1, :]
            for s in range(t + 1):
                mixed = mixed + ws_ref[t, s:s + 1, :] * vg_ref[s * slab:(s + 1) * slab, :].astype(F32)
            og_scr[t * slab:(t + 1) * slab, :] = u_ref[t * slab:(t + 1) * slab, :] * mixed
    else:
        tri = (lax.broadcasted_iota(jnp.int32, (CHUNK, CHUNK), 0)
               >= lax.broadcasted_iota(jnp.int32, (CHUNK, CHUNK), 1))
        for gi in range(n_groups):
            sl = slice(gi * GROUP_DIM, (gi + 1) * GROUP_DIM)
            w = jnp.where(tri, ws_ref[gi], 0.0).astype(BF16)
            for c in range(tm // CHUNK):
                rs = slice(c * CHUNK, (c + 1) * CHUNK)
                mixed = jnp.dot(w, vg_ref[rs, sl].astype(BF16), preferred_element_type=F32) + bs_ref[:, sl]
                og_scr[rs, sl] = u_ref[rs, sl] * mixed
    pm = min(tm, 256)
    gt, sc, sh = (_tile_rows(m_ref[...], pm) for m_ref in (gt_ref, sc_ref, sh_ref))
    for r in range(tm // pm):
        rows = slice(r * pm, (r + 1) * pm)
        a_n = _rms(oa_ref[rows, :], ga_ref[...]).astype(BF16)
        g_n = _rms(og_scr[rows, :], gg_ref[...]).astype(BF16)
        out = (jnp.dot(a_n, wo_ref[0:da, :], preferred_element_type=F32)
               + jnp.dot(g_n, wo_ref[da:, :], preferred_element_type=F32))
        x1 = x_ref[rows, :] + gt * out
        x1_ref[rows, :] = x1
        h2 = _rms(x1, gf_ref[...]) * (1.0 + sc) + sh
        h2_ref[rows, :] = h2.astype(h2_ref.dtype)


def _outproj(oa, u, vg, x, mod, g_oa, g_og, g_ffn, w_out_bf, ws, bs, *, tm, tiles_per_batch, time_major):
    rows, d = x.shape
    da = oa.shape[1]
    r = mod.shape[1]
    modspec = lambda m: pl.BlockSpec((None, r, d), lambda i, m=m: (i // tiles_per_batch, 0, m))
    row = lambda n: pl.BlockSpec((tm, n), lambda i: (i, 0))
    full = lambda a: pl.BlockSpec(a.shape, lambda i, nd=a.ndim: (0,) * nd, pipeline_mode=pl.Buffered(1))
    g_oa, g_og, g_ffn = g_oa.reshape(1, -1), g_og.reshape(1, -1), g_ffn.reshape(1, -1)
    return pl.pallas_call(
        functools.partial(_outproj_kernel, time_major=time_major, n_groups=da // GROUP_DIM),
        grid=(rows // tm,),
        in_specs=[row(da), row(da), row(da), row(d), modspec(2), modspec(4), modspec(3),
                  full(g_oa), full(g_og), full(g_ffn), full(w_out_bf), full(ws), full(bs)],
        out_specs=[row(d), row(d)],
        out_shape=[jax.ShapeDtypeStruct((rows, d), F32), jax.ShapeDtypeStruct((rows, d), BF16)],
        scratch_shapes=[pltpu.VMEM((tm, da), F32)],
        compiler_params=_cparams(("arbitrary",)),
        name="gmlp_out_proj",
    )(oa, u, vg, x, mod, mod, mod, g_oa, g_og, g_ffn, w_out_bf, ws, bs)


def _ffn_kernel(h_ref, x1_ref, gt_ref, wa_ref, wp_ref, wc_ref, bc_ref, wd_ref, prev_ref,
                y_ref, conv_ref, abuf, halo, acc, *, shift, tiles_per_batch, n_parts):
    i = pl.program_id(0)
    f = pl.program_id(1)
    nf = pl.num_programs(1)
    tm = h_ref.shape[0]
    keep = (CONV_W - 1) * shift
    off = abuf.shape[0] - tm

    @pl.when(i % tiles_per_batch == 0)
    def _():
        abuf[off - keep:off, :] = prev_ref[...]

    @pl.when(i % tiles_per_batch != 0)
    def _():
        abuf[off - keep:off, :] = halo[f]

    @pl.when(f == 0)
    def _():
        acc[...] = jnp.zeros(acc.shape, F32)

    hm = tm // n_parts
    a_parts, p_parts = [], []
    for r in range(n_parts):
        rows = slice(r * hm, (r + 1) * hm)
        h = h_ref[rows, :]
        a = jnp.dot(h, wa_ref[...], preferred_element_type=F32)
        abuf[off + r * hm:off + (r + 1) * hm, :] = a
        a_parts.append(a)
        p_parts.append(jnp.dot(h, wp_ref[...], preferred_element_type=F32))
    for r in range(n_parts):
        rows = slice(r * hm, (r + 1) * hm)
        ac = bc_ref[...] + wc_ref[CONV_W - 1:CONV_W, :] * a_parts[r]
        for jj in range(CONV_W - 1):
            start = off + r * hm - (CONV_W - 1 - jj) * shift
            ac = ac + wc_ref[jj:jj + 1, :] * abuf[start:start + hm, :]
        gate = (ac * jax.nn.sigmoid(ac)) * p_parts[r]
        acc[rows, :] += jnp.dot(gate.astype(BF16), wd_ref[...], preferred_element_type=F32)
    last = abuf[off + tm - keep:off + tm, :]
    halo[f] = last
    conv_ref[...] = last

    @pl.when(f == nf - 1)
    def _():
        y_ref[...] = x1_ref[...] + _tile_rows(gt_ref[...], tm) * acc[...]


def _ffn(h2, x1, mod, w_up_bf, w_conv, b_conv, w_down_bf, conv_prev, *, tm, tf, tiles_per_batch, shift):
    rows, d = x1.shape
    ff = w_down_bf.shape[0]
    nf = ff // tf
    keep = conv_prev.shape[1]
    r = mod.shape[1]
    off = -(-keep // SUBLANES) * SUBLANES
    n_tiles = rows // tm
    y, tail = pl.pallas_call(
        functools.partial(_ffn_kernel, shift=shift, tiles_per_batch=tiles_per_batch, n_parts=2),
        grid=(n_tiles, nf),
        in_specs=[pl.BlockSpec((tm, d), lambda i, f: (i, 0)),
                  pl.BlockSpec((tm, d), lambda i, f: (i, 0)),
                  pl.BlockSpec((None, r, d), lambda i, f: (i // tiles_per_batch, 0, 5)),
                  pl.BlockSpec((d, tf), lambda i, f: (0, f)),
                  pl.BlockSpec((d, tf), lambda i, f: (0, f + nf)),
                  pl.BlockSpec((CONV_W, tf), lambda i, f: (0, f)),
                  pl.BlockSpec((1, tf), lambda i, f: (0, f)),
                  pl.BlockSpec((tf, d), lambda i, f: (f, 0)),
                  pl.BlockSpec((None, keep, tf), lambda i, f: (i // tiles_per_batch, 0, f))],
        out_specs=[pl.BlockSpec((tm, d), lambda i, f: (i, 0)),
                   pl.BlockSpec((None, keep, tf), lambda i, f: (i, 0, f))],
        out_shape=[jax.ShapeDtypeStruct((rows, d), F32), jax.ShapeDtypeStruct((n_tiles, keep, ff), F32)],
        scratch_shapes=[pltpu.VMEM((off + tm, tf), F32), pltpu.VMEM((nf, keep, tf), F32),
                        pltpu.VMEM((tm, d), F32)],
        compiler_params=_cparams(("arbitrary", "arbitrary")),
        name="conv_ffn",
    )(h2, x1, mod, w_up_bf, w_up_bf, w_conv, b_conv.reshape(1, ff), w_down_bf, conv_prev)
    return y, tail[tiles_per_batch - 1::tiles_per_batch]


def kernel(x_prompt, x_sample, cache_k, cache_v, state_conv, page_table, c_prompt, c_sample, w_ada, b_ada, g_norm_mix, w_in, g_q, g_k, g_v, w_s, b_s, g_out_attn, g_out_gmlp, w_out, g_norm_ffn, w_up, w_conv, b_conv, w_down):
    depth = w_ada.shape[0]
    b, s, d = x_prompt.shape
    db, n_q, _ = x_sample.shape
    page, n_heads, hd = cache_k.shape[2:]
    da = n_heads * hd
    ff = w_down.shape[1]
    n_groups = w_s.shape[1]
    past_len = page_table.shape[1] * page
    assert hd == HEAD_DIM and da == n_groups * GROUP_DIM and s % CHUNK == 0 and n_q <= CHUNK
    assert w_in.shape[2] == 5 * da
    slopes_np = (2.0 ** (-8.0 * np.arange(1, n_heads + 1) / n_heads)).astype(np.float32)
    slopes = jnp.asarray(slopes_np)

    n_c = b + db
    pad_c = -(-n_c // SUBLANES) * SUBLANES
    c_all = jnp.concatenate([c_prompt, c_sample, jnp.zeros((pad_c - n_c, d), F32)], axis=0)

    h_p = x_prompt.reshape(b * s, d)
    h_s = x_sample.transpose(1, 0, 2).reshape(n_q * db, d)
    tm_p = 512
    outs = [[] for _ in range(7)]
    for layer in range(depth):
        w_in_bf = w_in[layer].astype(BF16)
        w_out_bf = w_out[layer].astype(BF16)
        w_up_bf = w_up[layer].astype(BF16)
        w_down_bf = w_down[layer].astype(BF16)

        mod = _ada(c_all, w_ada[layer], b_ada[layer])
        mod_p = mod[:b].reshape(b, 1, N_MOD * d)
        mod_s = mod[b:n_c].reshape(1, db, N_MOD * d)

        q, k, v, u, vg = _inproj(h_p, mod_p, g_norm_mix[layer], w_in_bf, g_q[layer], g_k[layer], g_v[layer],
                                 tm=tm_p, tiles_per_batch=s // tm_p, q_dtype=BF16, vg_dtype=BF16)
        o_attn = _moba_prompt(q.reshape(b, s, da), k.reshape(b, s, da), v.reshape(b, s, da), slopes)
        bs_rows = jnp.repeat(b_s[layer].T, GROUP_DIM, axis=1)
        x1, h2 = _outproj(o_attn.reshape(b * s, da), u, vg, h_p, mod_p, g_out_attn[layer], g_out_gmlp[layer],
                          g_norm_ffn[layer], w_out_bf, w_s[layer], bs_rows,
                          tm=tm_p, tiles_per_batch=s // tm_p, time_major=False)
        conv0 = jnp.zeros((b, CONV_W - 1, ff), F32)
        y_p, conv_p = _ffn(h2, x1, mod_p, w_up_bf, w_conv[layer], b_conv[layer], w_down_bf, conv0,
                           tm=tm_p, tf=512, tiles_per_batch=s // tm_p, shift=1)

        qs, ks, vs, us, vgs = _inproj(h_s, mod_s, g_norm_mix[layer], w_in_bf, g_q[layer], g_k[layer], g_v[layer],
                                      tm=db, tiles_per_batch=n_q, q_dtype=F32, vg_dtype=F32)
        seq_major = lambda a: a.reshape(n_q, db, -1).transpose(1, 0, 2)
        per_head = lambda a: seq_major(a).reshape(db, n_q * n_heads, hd)
        ks_h, vs_h = per_head(ks), per_head(vs)
        o_s = _moba_sample(per_head(qs), ks_h, vs_h, cache_k, cache_v, layer, page_table, slopes_np, past_len)
        o_s = o_s.reshape(db, n_q, da).transpose(1, 0, 2).reshape(n_q * db, da)
        ws_t = jnp.repeat(jnp.tril(w_s[layer][:, :n_q, :n_q]).transpose(1, 2, 0), GROUP_DIM, axis=2)
        bs_t = jnp.repeat(b_s[layer][:, :n_q].T, GROUP_DIM, axis=1)
        x1s, h2s = _outproj(o_s, us, vgs, h_s, mod_s, g_out_attn[layer], g_out_gmlp[layer], g_norm_ffn[layer],
                            w_out_bf, ws_t, bs_t, tm=n_q * db, tiles_per_batch=1, time_major=True)
        prev_s = state_conv[layer].transpose(1, 0, 2).reshape(1, (CONV_W - 1) * db, ff)
        y_s, conv_s = _ffn(h2s, x1s, mod_s, w_up_bf, w_conv[layer], b_conv[layer], w_down_bf, prev_s,
                           tm=n_q * db, tf=512, tiles_per_batch=1, shift=db)

        h_p, h_s = y_p, y_s
        outs[0].append(k.reshape(b, s // page, page, n_heads, hd))
        outs[1].append(v.reshape(b, s // page, page, n_heads, hd))
        outs[2].append(ks_h.reshape(db, n_q, n_heads, hd))
        outs[3].append(vs_h.reshape(db, n_q, n_heads, hd))
        outs[4].append(conv_p)
        outs[5].append(conv_s.reshape(CONV_W - 1, db, ff).transpose(1, 0, 2))
        outs[6].append(seq_major(vgs))

    y_prompt = h_p.reshape(b, s, d)
    y_sample = h_s.reshape(n_q, db, d).transpose(1, 0, 2)
    return (y_prompt, y_sample) + tuple(jnp.stack(o) for o in outs)
```

```python
import functools
import math

import numpy as np
import jax
import jax.numpy as jnp
from jax import lax
from jax.experimental import pallas as pl
from jax.experimental.pallas import tpu as pltpu

F32 = jnp.float32
BF16 = jnp.bfloat16

HEAD_DIM = 128
GROUP_DIM = 128
MOBA_BLOCK = 256
MOBA_TOPK = 3
CHUNK = 128
CONV_W = 3
N_MOD = 6
EPS = 1e-6
NEG = -1e30
LOG2E = math.log2(math.e)
LANES = 128
SUBLANES = 8
FFN_PART_ROWS = 256
SCORE_ROWS = 64
VMEM_LIMIT = 56 * 1024 * 1024

_NT = (((1,), (1,)), ((), ()))


def _cparams(sem):
    return pltpu.CompilerParams(dimension_semantics=sem, vmem_limit_bytes=VMEM_LIMIT)


def _rms(x, g):
    ms = jnp.mean(x * x, axis=-1, keepdims=True)
    return (x * lax.rsqrt(ms + EPS)) * g


def _tile_rows(m, rows):
    r = m.shape[0]
    if r == 1 or r == rows:
        return m
    return jnp.concatenate([m] * (rows // r), axis=0)


def _topk_select(g, valid, n_blocks, axis):
    blk = lax.broadcasted_iota(jnp.int32, g.shape, axis)
    gm = jnp.where(valid, g, -jnp.inf)
    rank = jnp.zeros(g.shape, jnp.int32)
    for m in range(n_blocks):
        one = gm[m:m + 1, :] if axis == 0 else gm[:, m:m + 1]
        beats = (one > gm) | ((one == gm) & (m < blk))
        rank = rank + beats.astype(jnp.int32)
    return valid & (rank < MOBA_TOPK)


def _ada_kernel(c_ref, b_ref, *refs):
    w_refs, o_ref = refs[:-1], refs[-1]
    c = c_ref[...]
    s = (c * jax.nn.sigmoid(c)).astype(BF16)
    tw = w_refs[0].shape[1]
    for q, w_ref in enumerate(w_refs):
        cols = slice(q * tw, (q + 1) * tw)
        o_ref[:, cols] = jnp.dot(s, w_ref[...].astype(BF16), preferred_element_type=F32) + b_ref[:, cols]


def _ada(c, w, b):
    m, d = c.shape
    n = w.shape[1]
    tn, strips = 2048, 4
    tw = tn // strips
    return pl.pallas_call(
        _ada_kernel,
        grid=(n // tn,),
        in_specs=[pl.BlockSpec((m, d), lambda j: (0, 0)),
                  pl.BlockSpec((1, tn), lambda j: (0, j))]
                 + [pl.BlockSpec((d, tw), lambda j, q=q: (0, j * strips + q)) for q in range(strips)],
        out_specs=pl.BlockSpec((m, tn), lambda j: (0, j)),
        out_shape=jax.ShapeDtypeStruct((m, n), F32),
        compiler_params=_cparams(("arbitrary",)),
        name="ada_mod",
    )(c, b.reshape(1, n), *([w] * strips))


def _inproj_kernel(x_ref, sc_ref, sh_ref, gn_ref, w_ref, gq_ref, gk_ref, gv_ref,
                   q_ref, k_ref, v_ref, u_ref, vg_ref, h_scr, *, n_heads):
    tm = x_ref.shape[0]
    y = _rms(x_ref[...], gn_ref[...])
    h = y * (1.0 + _tile_rows(sc_ref[...], tm)) + _tile_rows(sh_ref[...], tm)
    h_scr[...] = h.astype(BF16)

    tn = n_heads * HEAD_DIM
    cw = 2 * HEAD_DIM
    for part, out_ref in enumerate((q_ref, k_ref, v_ref, u_ref, vg_ref)):
        for c0 in range(0, tn, cw):
            acc = jnp.dot(h_scr[...], w_ref[:, part * tn + c0:part * tn + c0 + cw], preferred_element_type=F32)
            for hh in range(cw // HEAD_DIM):
                sl = slice(c0 + hh * HEAD_DIM, c0 + (hh + 1) * HEAD_DIM)
                a = acc[:, hh * HEAD_DIM:(hh + 1) * HEAD_DIM]
                if part == 0:
                    a = _rms(a, gq_ref[...])
                elif part == 1:
                    a = _rms(a, gk_ref[...])
                elif part == 3:
                    a = jax.nn.gelu(a)
                elif part == 4:
                    a = _rms(jax.nn.gelu(a), gv_ref[:, sl])
                out_ref[:, sl] = a.astype(out_ref.dtype)


def _inproj(x, mod, g_norm, w_in_bf, g_q, g_k, g_v, *, tm, tiles_per_batch, q_dtype, vg_dtype, col_slabs=False):
    d = g_norm.shape[0]
    n_tot = w_in_bf.shape[1]
    tn = n_tot // 5
    n_heads = tn // HEAD_DIM
    r = min(mod.shape[1], tm)
    if col_slabs:
        n_tiles = x.shape[1] // d
        tile_at = lambda i: (0, i)
        out_dims = (tm, n_tiles * tn)
    else:
        n_tiles = x.shape[0] // tm
        tile_at = lambda i: (i, 0)
        out_dims = (x.shape[0], tn)
    modspec = lambda m: pl.BlockSpec((None, r, d), lambda i, m=m: (i // tiles_per_batch, 0, m))
    rowspec = pl.BlockSpec((tm, tn), tile_at)
    vec = lambda n: pl.BlockSpec((1, n), lambda i: (0, 0))
    outs = pl.pallas_call(
        functools.partial(_inproj_kernel, n_heads=n_heads),
        grid=(n_tiles,),
        in_specs=[pl.BlockSpec((tm, d), tile_at),
                  modspec(1), modspec(0), vec(d),
                  pl.BlockSpec((d, n_tot), lambda i: (0, 0), pipeline_mode=pl.Buffered(1)),
                  vec(HEAD_DIM), vec(HEAD_DIM), vec(tn)],
        out_specs=[rowspec] * 5,
        out_shape=[jax.ShapeDtypeStruct(out_dims, dt) for dt in (q_dtype, F32, F32, F32, vg_dtype)],
        scratch_shapes=[pltpu.VMEM((tm, d), BF16)],
        compiler_params=_cparams(("arbitrary",)),
        name="in_proj",
    )(x, mod, mod, g_norm.reshape(1, d), w_in_bf, g_q.reshape(1, -1), g_k.reshape(1, -1), g_v.reshape(1, -1))
    return outs


def _moba_prompt_kernel(slopes_ref, q_ref, k_ref, v_ref, o_ref, kb_scr, vb_scr, km_scr, al_scr,
                        qa_scr, raw_scr, p_scr, m_scr, l_scr, a_scr, acc_scr, *, n_blocks, heads_per_step):
    hg = pl.program_id(1)
    qi = pl.program_id(2)
    blk = MOBA_BLOCK
    tile = 2 * blk
    c1 = HEAD_DIM ** -0.5 * LOG2E
    heads = range(heads_per_step)
    nslope = [-slopes_ref[hg * heads_per_step + e] * LOG2E for e in heads]

    @pl.when(qi == 0)
    def _():
        km_scr[...] = jnp.zeros(km_scr.shape, F32)
        lane = lax.broadcasted_iota(jnp.int32, (blk, LANES), 1)
        rel = (lax.broadcasted_iota(jnp.int32, (tile, tile), 0)
               - lax.broadcasted_iota(jnp.int32, (tile, tile), 1)).astype(F32)
        for e in heads:
            cols = slice(e * HEAD_DIM, (e + 1) * HEAD_DIM)
            for n in range(n_blocks):
                rows = slice(n * blk, (n + 1) * blk)
                kblk = k_ref[rows, cols]
                kb_scr[e, rows, 0:HEAD_DIM] = kblk.astype(BF16)
                kb_scr[e, rows, HEAD_DIM:] = jnp.where(lane == n, 1.0, 0.0).astype(BF16)
                vb_scr[e, rows, :] = v_ref[rows, cols].astype(BF16)
                km_scr[e, n:n + 1, :] = jnp.mean(kblk, axis=0, keepdims=True)
            al_scr[e] = nslope[e] * rel

    def augmented_query(e):
        q = q_ref[:, e * HEAD_DIM:(e + 1) * HEAD_DIM]
        g = lax.dot_general(km_scr[e].astype(BF16), q, _NT, preferred_element_type=F32)[:n_blocks]
        bidx = lax.broadcasted_iota(jnp.int32, g.shape, 0)
        own = 2 * qi + lax.broadcasted_iota(jnp.int32, g.shape, 1) // blk
        sel = _topk_select(g, bidx < own, n_blocks, axis=0)
        bias = jnp.where(sel | (bidx == own), 0.0, NEG).astype(F32)
        bias = jnp.concatenate([bias, jnp.full((LANES - n_blocks, tile), NEG, F32)], axis=0)
        return jnp.concatenate([q, bias.T.astype(BF16)], axis=1)

    for e in heads:
        qa_scr[e] = augmented_query(e)

    rc = SCORE_ROWS
    coli = lax.broadcasted_iota(jnp.int32, (rc, tile), 1)
    rowi = lax.broadcasted_iota(jnp.int32, (rc, tile), 0)

    def score(e, t):
        start = pl.multiple_of(t * tile, tile)
        raw_scr[e] = lax.dot_general(qa_scr[e], kb_scr[e, pl.ds(start, tile), :], _NT,
                                     preferred_element_type=F32)

    def fold(e, t, diagonal):
        start = pl.multiple_of(t * tile, tile)
        off = nslope[e] * ((qi - t) * tile).astype(F32)
        for c in range(tile // rc):
            rows = slice(c * rc, (c + 1) * rc)
            s = raw_scr[e, rows, :] * c1 + al_scr[e, rows, :]
            if diagonal:
                s = jnp.where(coli <= rowi + c * rc, s, NEG)
                m_new = jnp.broadcast_to(jnp.max(s, axis=1, keepdims=True) + off, (rc, LANES))
            else:
                m_old = m_scr[e, rows, :]
                m_new = jnp.maximum(m_old, jnp.max(s, axis=1, keepdims=True) + off)
                alpha = jnp.exp2(m_old - m_new)
                a_scr[e, rows, :] = alpha
            p = jnp.exp2(s - jnp.concatenate([m_new - off] * (tile // LANES), axis=1))
            psum = jnp.sum(p, axis=1, keepdims=True)
            p_scr[e, rows, :] = p.astype(BF16)
            m_scr[e, rows, :] = m_new
            if diagonal:
                l_scr[e, rows, :] = jnp.broadcast_to(psum, (rc, LANES))
            else:
                l_scr[e, rows, :] = alpha * l_scr[e, rows, :] + psum
        pv = jnp.dot(p_scr[e], vb_scr[e, pl.ds(start, tile), :], preferred_element_type=F32)
        if diagonal:
            acc_scr[e] = pv
        else:
            acc_scr[e] = a_scr[e] * acc_scr[e] + pv

    def visit(t, diagonal):
        for e in heads:
            score(e, t)
        for e in heads:
            fold(e, t, diagonal)

    visit(qi, True)

    def body(t, carry):
        visit(t, False)
        return carry

    lax.fori_loop(0, qi, body, 0)
    for e in heads:
        o_ref[:, e * HEAD_DIM:(e + 1) * HEAD_DIM] = (acc_scr[e] / l_scr[e]).astype(o_ref.dtype)


def _moba_prompt(q, k, v, slopes, heads_per_step=2):
    b, s, da = k.shape
    n_heads = da // HEAD_DIM
    n_blocks = s // MOBA_BLOCK
    hw = heads_per_step * HEAD_DIM
    assert s % (2 * MOBA_BLOCK) == 0 and n_blocks >= MOBA_TOPK and n_blocks <= LANES
    assert n_heads % heads_per_step == 0
    tile = 2 * MOBA_BLOCK
    qspec = pl.BlockSpec((None, tile, hw), lambda bb, hh, qi: (bb, qi, hh))
    kvspec = pl.BlockSpec((None, s, hw), lambda bb, hh, qi: (bb, 0, hh))
    return pl.pallas_call(
        functools.partial(_moba_prompt_kernel, n_blocks=n_blocks, heads_per_step=heads_per_step),
        grid=(b, n_heads // heads_per_step, s // tile),
        in_specs=[pl.BlockSpec(memory_space=pltpu.SMEM), qspec, kvspec, kvspec],
        out_specs=qspec,
        out_shape=jax.ShapeDtypeStruct((b, s, da), F32),
        scratch_shapes=[pltpu.VMEM((heads_per_step, s, 2 * HEAD_DIM), BF16),
                        pltpu.VMEM((heads_per_step, s, HEAD_DIM), BF16),
                        pltpu.VMEM((heads_per_step, LANES, HEAD_DIM), F32),
                        pltpu.VMEM((heads_per_step, tile, tile), F32),
                        pltpu.VMEM((heads_per_step, tile, 2 * HEAD_DIM), BF16),
                        pltpu.VMEM((heads_per_step, tile, tile), F32),
                        pltpu.VMEM((heads_per_step, tile, tile), BF16),
                        pltpu.VMEM((heads_per_step, tile, LANES), F32),
                        pltpu.VMEM((heads_per_step, tile, LANES), F32),
                        pltpu.VMEM((heads_per_step, tile, LANES), F32),
                        pltpu.VMEM((heads_per_step, tile, HEAD_DIM), F32)],
        compiler_params=_cparams(("arbitrary", "arbitrary", "arbitrary")),
        name="moba_prompt",
    )(slopes, q, k, v)


def _moba_sample_kernel(pt_ref, q_ref, kn_ref, vn_ref, slope_ref, *refs, n_heads, n_q, past_len, n_blocks):
    del pt_ref
    n_pages = 2 * n_blocks
    k_refs, v_refs, o_ref = refs[:n_pages], refs[n_pages:2 * n_pages], refs[2 * n_pages]
    nrow = n_q * n_heads
    scale = HEAD_DIM ** -0.5
    page = k_refs[0].shape[0]
    pkeys = page * n_heads
    lane128 = lax.broadcasted_iota(jnp.int32, (nrow, LANES), 1)
    rowc = lax.broadcasted_iota(jnp.int32, (nrow, 1), 0)
    tq = past_len + rowc // n_heads
    slope = slope_ref[...]
    qb = q_ref[...].astype(BF16)
    qf = qb.astype(F32)

    def masked_scores(raw, pos0, width, extra_ok=None):
        lane = lax.broadcasted_iota(jnp.int32, (nrow, width), 1)
        kpos = pos0 + lane // n_heads
        s = raw * scale - slope * (tq - kpos).astype(F32)
        ok = (lane % n_heads) == (rowc % n_heads)
        if extra_ok is not None:
            ok = ok & extra_ok(lane, kpos)
        return jnp.where(ok, s, NEG)

    g_all = jnp.zeros((nrow, LANES), F32)
    m_all = jnp.zeros((nrow, LANES), F32)
    l_all = jnp.zeros((nrow, LANES), F32)
    zq = jnp.zeros_like(qb)
    q_pair = jnp.concatenate([jnp.concatenate([qb, zq], axis=1), jnp.concatenate([zq, qb], axis=1)], axis=0)
    as_keys = lambda ref: ref[...].reshape(pkeys, HEAD_DIM).astype(BF16)
    raws = []
    for n in range(n_blocks):
        k0 = k_refs[2 * n][...]
        k1 = k_refs[2 * n + 1][...]
        kk = jnp.concatenate([k0.reshape(pkeys, HEAD_DIM).astype(BF16),
                              k1.reshape(pkeys, HEAD_DIM).astype(BF16)], axis=1)
        raws.append(lax.dot_general(q_pair, kk, _NT, preferred_element_type=F32))
        kmean = ((jnp.sum(k0, axis=0) + jnp.sum(k1, axis=0)) / MOBA_BLOCK).astype(BF16).astype(F32)
        g_n = jnp.sum(qf * jnp.concatenate([kmean] * n_q, axis=0), axis=1, keepdims=True)
        g_all = jnp.where(lane128 == n, g_n, g_all)
    pps = []
    for n in range(n_blocks):
        s0 = masked_scores(raws[n][:nrow], n * MOBA_BLOCK, pkeys)
        s1 = masked_scores(raws[n][nrow:], n * MOBA_BLOCK + page, pkeys)
        m_n = jnp.maximum(jnp.max(s0, axis=1, keepdims=True), jnp.max(s1, axis=1, keepdims=True))
        p0 = jnp.exp(s0 - m_n)
        p1 = jnp.exp(s1 - m_n)
        l_n = jnp.sum(p0, axis=1, keepdims=True) + jnp.sum(p1, axis=1, keepdims=True)
        pps.append(jnp.concatenate([p0, p1], axis=0).astype(BF16))
        m_all = jnp.where(lane128 == n, m_n, m_all)
        l_all = jnp.where(lane128 == n, l_n, l_all)
    outs = []
    for n in range(n_blocks):
        vv = jnp.concatenate([as_keys(v_refs[2 * n]), as_keys(v_refs[2 * n + 1])], axis=1)
        o2 = jnp.dot(pps[n], vv, preferred_element_type=F32)
        outs.append(o2[:nrow, :HEAD_DIM] + o2[nrow:, HEAD_DIM:])

    sel = _topk_select(g_all, lane128 < n_blocks, n_blocks, axis=1)
    pad = jnp.zeros((LANES - nrow, HEAD_DIM), F32)
    own_k = jnp.concatenate([kn_ref[...], pad], axis=0).astype(BF16)
    own_v = jnp.concatenate([vn_ref[...], pad], axis=0).astype(BF16)
    s_own = masked_scores(lax.dot_general(qb, own_k, _NT, preferred_element_type=F32), past_len, LANES,
                          lambda lane, kpos: (lane < nrow) & (kpos <= tq))
    m_fin = jnp.maximum(jnp.max(jnp.where(sel, m_all, NEG), axis=1, keepdims=True),
                        jnp.max(s_own, axis=1, keepdims=True))
    w = jnp.where(sel, jnp.exp(jnp.where(sel, m_all - m_fin, 0.0)), 0.0)
    p_own = jnp.exp(s_own - m_fin)
    l_fin = jnp.sum(w * l_all, axis=1, keepdims=True) + jnp.sum(p_own, axis=1, keepdims=True)
    comb = jnp.dot(p_own.astype(BF16), own_v, preferred_element_type=F32)
    for n in range(n_blocks):
        comb = comb + w[:, n:n + 1] * outs[n]
    o_ref[...] = (comb / l_fin).astype(o_ref.dtype)


def _moba_sample(q, k_new, v_new, cache_k, cache_v, layer, page_table, slopes_np, past_len):
    db, nrow, hd = q.shape
    page, n_heads = cache_k.shape[2:4]
    n_q = nrow // n_heads
    n_pages = page_table.shape[1]
    n_blocks = past_len // MOBA_BLOCK
    assert past_len % MOBA_BLOCK == 0 and MOBA_BLOCK == 2 * page and n_pages * page == past_len
    assert n_blocks >= MOBA_TOPK and n_blocks <= LANES and n_heads == SUBLANES
    assert nrow % 16 == 0 and nrow <= LANES
    slope_col = np.tile(slopes_np, n_q).reshape(nrow, 1)
    seqspec = pl.BlockSpec((None, nrow, hd), lambda s, pt: (s, 0, 0))
    pagespecs = [pl.BlockSpec((None, None, page, n_heads, hd),
                              lambda s, pt, j=j: (layer, pt[s * n_pages + j], 0, 0, 0))
                 for j in range(n_pages)]
    grid_spec = pltpu.PrefetchScalarGridSpec(
        num_scalar_prefetch=1,
        grid=(db,),
        in_specs=[seqspec, seqspec, seqspec, pl.BlockSpec((nrow, 1), lambda s, pt: (0, 0))] + pagespecs * 2,
        out_specs=seqspec,
    )
    return pl.pallas_call(
        functools.partial(_moba_sample_kernel, n_heads=n_heads, n_q=n_q, past_len=past_len, n_blocks=n_blocks),
        grid_spec=grid_spec,
        out_shape=jax.ShapeDtypeStruct((db, nrow, hd), F32),
        compiler_params=_cparams(("arbitrary",)),
        name="moba_sample",
    )(page_table.reshape(-1), q, k_new, v_new, jnp.asarray(slope_col),
      *([cache_k] * n_pages), *([cache_v] * n_pages))


def _outproj_kernel(oa_ref, u_ref, vg_ref, x_ref, gt_ref, sc_ref, sh_ref, ga_ref, gg_ref, gf_ref,
                    wo_ref, ws_ref, bs_ref, x1_ref, h2_ref, og_scr, *, col_slabs, n_groups):
    tm = x_ref.shape[0]
    da = n_groups * GROUP_DIM
    d = gf_ref.shape[1]
    if col_slabs:
        n_t = ws_ref.shape[0]
        for t in range(n_t):
            mixed = jnp.zeros((tm, da), F32) + bs_ref[t:t + 1, :]
            for s in range(t + 1):
                mixed = mixed + ws_ref[t, s:s + 1, :] * vg_ref[:, s * da:(s + 1) * da].astype(F32)
            og_scr[:, t * da:(t + 1) * da] = u_ref[:, t * da:(t + 1) * da] * mixed
        parts = [(slice(None), t) for t in range(n_t)]
    else:
        pm = min(tm, 256)
        parts = [(slice(r * pm, (r + 1) * pm), 0) for r in range(tm // pm)]
        tri = (lax.broadcasted_iota(jnp.int32, (CHUNK, CHUNK), 0)
               >= lax.broadcasted_iota(jnp.int32, (CHUNK, CHUNK), 1))
        for gi in range(n_groups):
            sl = slice(gi * GROUP_DIM, (gi + 1) * GROUP_DIM)
            w = jnp.where(tri, ws_ref[gi], 0.0).astype(BF16)
            for c in range(tm // CHUNK):
                rs = slice(c * CHUNK, (c + 1) * CHUNK)
                mixed = jnp.dot(w, vg_ref[rs, sl].astype(BF16), preferred_element_type=F32) + bs_ref[:, sl]
                og_scr[rs, sl] = u_ref[rs, sl] * mixed
    gt, sc, sh = gt_ref[...], sc_ref[...], sh_ref[...]
    for rows, t in parts:
        ca, cd = slice(t * da, (t + 1) * da), slice(t * d, (t + 1) * d)
        a_n = _rms(oa_ref[rows, ca], ga_ref[...]).astype(BF16)
        g_n = _rms(og_scr[rows, ca], gg_ref[...]).astype(BF16)
        out = (jnp.dot(a_n, wo_ref[0:da, :], preferred_element_type=F32)
               + jnp.dot(g_n, wo_ref[da:, :], preferred_element_type=F32))
        x1 = x_ref[rows, cd] + gt * out
        x1_ref[rows, cd] = x1
        h2 = _rms(x1, gf_ref[...]) * (1.0 + sc) + sh
        h2_ref[rows, cd] = h2.astype(h2_ref.dtype)


def _outproj(oa, u, vg, x, mod, g_oa, g_og, g_ffn, w_out_bf, ws, bs, *, tm, tiles_per_batch, col_slabs):
    rows = x.shape[0]
    d = g_ffn.shape[0]
    da = g_oa.shape[0]
    r = min(mod.shape[1], tm)
    modspec = lambda m: pl.BlockSpec((None, r, d), lambda i, m=m: (i // tiles_per_batch, 0, m))
    row = lambda a: pl.BlockSpec((tm, a.shape[1]), lambda i: (i, 0))
    full = lambda a: pl.BlockSpec(a.shape, lambda i, nd=a.ndim: (0,) * nd, pipeline_mode=pl.Buffered(1))
    g_oa, g_og, g_ffn = g_oa.reshape(1, -1), g_og.reshape(1, -1), g_ffn.reshape(1, -1)
    return pl.pallas_call(
        functools.partial(_outproj_kernel, col_slabs=col_slabs, n_groups=da // GROUP_DIM),
        grid=(rows // tm,),
        in_specs=[row(oa), row(u), row(vg), row(x), modspec(2), modspec(4), modspec(3),
                  full(g_oa), full(g_og), full(g_ffn), full(w_out_bf), full(ws), full(bs)],
        out_specs=[row(x), row(x)],
        out_shape=[jax.ShapeDtypeStruct(x.shape, F32), jax.ShapeDtypeStruct(x.shape, BF16)],
        scratch_shapes=[pltpu.VMEM((tm, oa.shape[1]), F32)],
        compiler_params=_cparams(("arbitrary",)),
        name="gmlp_out_proj",
    )(oa, u, vg, x, mod, mod, mod, g_oa, g_og, g_ffn, w_out_bf, ws, bs)


def _ffn_kernel(h_ref, x1_ref, gt_ref, wa_ref, wp_ref, wc_ref, bc_ref, wd_ref, prev_ref,
                y_ref, conv_ref, abuf, halo, *stack, shift, tiles_per_batch, n_parts, n_slabs):
    i = pl.program_id(0)
    f = pl.program_id(1)
    nf = pl.num_programs(1)
    d = wd_ref.shape[1]
    keep = (CONV_W - 1) * shift
    off = -(-keep // SUBLANES) * SUBLANES
    tm = abuf.shape[0] - off
    if n_slabs > 1:
        h_src, = stack

        @pl.when(f == 0)
        def _():
            for t in range(n_slabs):
                h_src[t * shift:(t + 1) * shift, :] = h_ref[:, t * d:(t + 1) * d]
    else:
        h_src = h_ref

    @pl.when(i % tiles_per_batch == 0)
    def _():
        abuf[off - keep:off, :] = prev_ref[...]

    @pl.when(i % tiles_per_batch != 0)
    def _():
        abuf[off - keep:off, :] = halo[f]

    @pl.when(f == 0)
    def _():
        y_ref[...] = jnp.zeros(y_ref.shape, F32)

    def accumulate(row0, val):
        if n_slabs == 1:
            y_ref[row0:row0 + val.shape[0], :] += val
        else:
            for k in range(val.shape[0] // shift):
                t = row0 // shift + k
                y_ref[:, t * d:(t + 1) * d] += val[k * shift:(k + 1) * shift, :]

    hm = tm // n_parts
    a_parts, p_parts = [], []
    for r in range(n_parts):
        rows = slice(r * hm, (r + 1) * hm)
        h = h_src[rows, :]
        a = jnp.dot(h, wa_ref[...], preferred_element_type=F32)
        abuf[off + r * hm:off + (r + 1) * hm, :] = a
        a_parts.append(a)
        p_parts.append(jnp.dot(h, wp_ref[...], preferred_element_type=F32))
    for r in range(n_parts):
        rows = slice(r * hm, (r + 1) * hm)
        ac = bc_ref[...] + wc_ref[CONV_W - 1:CONV_W, :] * a_parts[r]
        for jj in range(CONV_W - 1):
            start = off + r * hm - (CONV_W - 1 - jj) * shift
            ac = ac + wc_ref[jj:jj + 1, :] * abuf[start:start + hm, :]
        gate = (ac * jax.nn.sigmoid(ac)) * p_parts[r]
        accumulate(r * hm, jnp.dot(gate.astype(BF16), wd_ref[...], preferred_element_type=F32))
    last = abuf[off + tm - keep:off + tm, :]
    halo[f] = last
    conv_ref[...] = last

    @pl.when(f == nf - 1)
    def _():
        gt = gt_ref[...]
        for t in range(n_slabs):
            cols = slice(t * d, (t + 1) * d)
            y_ref[:, cols] = x1_ref[:, cols] + gt * y_ref[:, cols]


def _ffn(h2, x1, mod, w_up_bf, w_conv, b_conv, w_down_bf, conv_prev, *, tm, tf, tiles_per_batch, shift,
         n_slabs=1):
    d = w_down_bf.shape[1]
    ff = w_down_bf.shape[0]
    nf = ff // tf
    keep = conv_prev.shape[1]
    off = -(-keep // SUBLANES) * SUBLANES
    n_tiles = 1 if n_slabs > 1 else x1.shape[0] // tm
    blk_rows = tm // n_slabs
    r = min(mod.shape[1], blk_rows)
    stack = [pltpu.VMEM((tm, d), BF16)] if n_slabs > 1 else []
    x1_mode = dict(pipeline_mode=pl.Buffered(1)) if n_tiles > 1 else {}
    y, tail = pl.pallas_call(
        functools.partial(_ffn_kernel, shift=shift, tiles_per_batch=tiles_per_batch,
                          n_parts=tm // FFN_PART_ROWS, n_slabs=n_slabs),
        grid=(n_tiles, nf),
        in_specs=[pl.BlockSpec((blk_rows, n_slabs * d), lambda i, f: (i, 0)),
                  pl.BlockSpec((blk_rows, n_slabs * d), lambda i, f: (i, 0), **x1_mode),
                  pl.BlockSpec((None, r, d), lambda i, f: (i // tiles_per_batch, 0, 5)),
                  pl.BlockSpec((d, tf), lambda i, f: (0, f)),
                  pl.BlockSpec((d, tf), lambda i, f: (0, f + nf)),
                  pl.BlockSpec((CONV_W, tf), lambda i, f: (0, f)),
                  pl.BlockSpec((1, tf), lambda i, f: (0, f)),
                  pl.BlockSpec((tf, d), lambda i, f: (f, 0)),
                  pl.BlockSpec((None, keep, tf), lambda i, f: (i // tiles_per_batch, 0, f))],
        out_specs=[pl.BlockSpec((blk_rows, n_slabs * d), lambda i, f: (i, 0)),
                   pl.BlockSpec((None, keep, tf), lambda i, f: (i, 0, f))],
        out_shape=[jax.ShapeDtypeStruct(x1.shape, F32), jax.ShapeDtypeStruct((n_tiles, keep, ff), F32)],
        scratch_shapes=[pltpu.VMEM((off + tm, tf), F32), pltpu.VMEM((nf, keep, tf), F32)] + stack,
        compiler_params=_cparams(("arbitrary", "arbitrary")),
        name="conv_ffn",
    )(h2, x1, mod, w_up_bf, w_up_bf, w_conv, b_conv.reshape(1, ff), w_down_bf, conv_prev)
    return y, tail[tiles_per_batch - 1::tiles_per_batch]


def kernel(x_prompt, x_sample, cache_k, cache_v, state_conv, page_table, c_prompt, c_sample, w_ada, b_ada, g_norm_mix, w_in, g_q, g_k, g_v, w_s, b_s, g_out_attn, g_out_gmlp, w_out, g_norm_ffn, w_up, w_conv, b_conv, w_down):
    depth = w_ada.shape[0]
    b, s, d = x_prompt.shape
    db, n_q, _ = x_sample.shape
    page, n_heads, hd = cache_k.shape[2:]
    da = n_heads * hd
    ff = w_down.shape[1]
    n_groups = w_s.shape[1]
    past_len = page_table.shape[1] * page
    assert hd == HEAD_DIM and da == n_groups * GROUP_DIM and s % CHUNK == 0 and n_q <= CHUNK
    assert w_in.shape[2] == 5 * da
    slopes_np = (2.0 ** (-8.0 * np.arange(1, n_heads + 1) / n_heads)).astype(np.float32)
    slopes = jnp.asarray(slopes_np)

    n_c = b + db
    pad_c = -(-n_c // SUBLANES) * SUBLANES
    c_all = jnp.concatenate([c_sample, c_prompt, jnp.zeros((pad_c - n_c, d), F32)], axis=0)

    h_p = x_prompt.reshape(b * s, d)
    h_s = x_sample.reshape(db, n_q * d)
    tm_p = 512
    tm_ffn = 1024
    outs = [[] for _ in range(7)]
    for layer in range(depth):
        w_in_bf = w_in[layer].astype(BF16)
        w_out_bf = w_out[layer].astype(BF16)
        w_up_bf = w_up[layer].astype(BF16)
        w_down_bf = w_down[layer].astype(BF16)

        mod = _ada(c_all, w_ada[layer], b_ada[layer])
        mod_p = mod[db:n_c].reshape(b, 1, N_MOD * d)
        mod_s = mod.reshape(1, pad_c, N_MOD * d)

        q, k, v, u, vg = _inproj(h_p, mod_p, g_norm_mix[layer], w_in_bf, g_q[layer], g_k[layer], g_v[layer],
                                 tm=tm_p, tiles_per_batch=s // tm_p, q_dtype=BF16, vg_dtype=BF16)
        o_attn = _moba_prompt(q.reshape(b, s, da), k.reshape(b, s, da), v.reshape(b, s, da), slopes)
        bs_rows = jnp.repeat(b_s[layer].T, GROUP_DIM, axis=1)
        x1, h2 = _outproj(o_attn.reshape(b * s, da), u, vg, h_p, mod_p, g_out_attn[layer], g_out_gmlp[layer],
                          g_norm_ffn[layer], w_out_bf, w_s[layer], bs_rows,
                          tm=tm_p, tiles_per_batch=s // tm_p, col_slabs=False)
        conv0 = jnp.zeros((b, CONV_W - 1, ff), F32)
        y_p, conv_p = _ffn(h2, x1, mod_p, w_up_bf, w_conv[layer], b_conv[layer], w_down_bf, conv0,
                           tm=tm_ffn, tf=512, tiles_per_batch=s // tm_ffn, shift=1)

        qs, ks, vs, us, vgs = _inproj(h_s, mod_s, g_norm_mix[layer], w_in_bf, g_q[layer], g_k[layer], g_v[layer],
                                      tm=db, tiles_per_batch=n_q, q_dtype=F32, vg_dtype=F32, col_slabs=True)
        per_head = lambda a: a.reshape(db, n_q * n_heads, hd)
        o_s = _moba_sample(per_head(qs), per_head(ks), per_head(vs), cache_k, cache_v, layer, page_table,
                           slopes_np, past_len).reshape(db, n_q * da)
        ws_t = jnp.repeat(jnp.tril(w_s[layer][:, :n_q, :n_q]).transpose(1, 2, 0), GROUP_DIM, axis=2)
        bs_t = jnp.repeat(b_s[layer][:, :n_q].T, GROUP_DIM, axis=1)
        x1s, h2s = _outproj(o_s, us, vgs, h_s, mod_s, g_out_attn[layer], g_out_gmlp[layer], g_norm_ffn[layer],
                            w_out_bf, ws_t, bs_t, tm=db, tiles_per_batch=1, col_slabs=True)
        prev_s = state_conv[layer].transpose(1, 0, 2).reshape(1, (CONV_W - 1) * db, ff)
        y_s, conv_s = _ffn(h2s, x1s, mod_s, w_up_bf, w_conv[layer], b_conv[layer], w_down_bf, prev_s,
                           tm=n_q * db, tf=512, tiles_per_batch=1, shift=db, n_slabs=n_q)

        h_p, h_s = y_p, y_s
        outs[0].append(k.reshape(b, s // page, page, n_heads, hd))
        outs[1].append(v.reshape(b, s // page, page, n_heads, hd))
        outs[2].append(ks.reshape(db, n_q, n_heads, hd))
        outs[3].append(vs.reshape(db, n_q, n_heads, hd))
        outs[4].append(conv_p)
        outs[5].append(conv_s.reshape(CONV_W - 1, db, ff).transpose(1, 0, 2))
        outs[6].append(vgs.reshape(db, n_q, da))

    y_prompt = h_p.reshape(b, s, d)
    y_sample = h_s.reshape(db, n_q, d)
    return (y_prompt, y_sample) + tuple(jnp.stack(o) for o in outs)
```

```python
import functools
import math

import numpy as np
import jax
import jax.numpy as jnp
from jax import lax
from jax.experimental import pallas as pl
from jax.experimental.pallas import tpu as pltpu

F32 = jnp.float32
BF16 = jnp.bfloat16

HEAD_DIM = 128
GROUP_DIM = 128
MOBA_BLOCK = 256
MOBA_TOPK = 3
CHUNK = 128
CONV_W = 3
N_MOD = 6
EPS = 1e-6
NEG = -1e30
LOG2E = math.log2(math.e)
LANES = 128
SUBLANES = 8
FFN_PART_ROWS = 256
SCORE_ROWS = 64
VMEM_LIMIT = 56 * 1024 * 1024

_NT = (((1,), (1,)), ((), ()))


def _cparams(sem):
    return pltpu.CompilerParams(dimension_semantics=sem, vmem_limit_bytes=VMEM_LIMIT)


def _rms(x, g):
    ms = jnp.mean(x * x, axis=-1, keepdims=True)
    return (x * lax.rsqrt(ms + EPS)) * g


def _tile_rows(m, rows):
    r = m.shape[0]
    if r == 1 or r == rows:
        return m
    return jnp.concatenate([m] * (rows // r), axis=0)


def _topk_select(g, valid, n_blocks, axis):
    blk = lax.broadcasted_iota(jnp.int32, g.shape, axis)
    gm = jnp.where(valid, g, -jnp.inf)
    rank = jnp.zeros(g.shape, jnp.int32)
    for m in range(n_blocks):
        one = gm[m:m + 1, :] if axis == 0 else gm[:, m:m + 1]
        beats = (one > gm) | ((one == gm) & (m < blk))
        rank = rank + beats.astype(jnp.int32)
    return valid & (rank < MOBA_TOPK)


def _ada_kernel(c_ref, b_ref, *refs):
    w_refs, o_ref = refs[:-1], refs[-1]
    c = c_ref[...]
    s = (c * jax.nn.sigmoid(c)).astype(BF16)
    tw = w_refs[0].shape[1]
    for q, w_ref in enumerate(w_refs):
        cols = slice(q * tw, (q + 1) * tw)
        o_ref[:, cols] = jnp.dot(s, w_ref[...].astype(BF16), preferred_element_type=F32) + b_ref[:, cols]


def _ada(c, w, b):
    m, d = c.shape
    n = w.shape[1]
    tn, strips = 2048, 4
    tw = tn // strips
    return pl.pallas_call(
        _ada_kernel,
        grid=(n // tn,),
        in_specs=[pl.BlockSpec((m, d), lambda j: (0, 0)),
                  pl.BlockSpec((1, tn), lambda j: (0, j))]
                 + [pl.BlockSpec((d, tw), lambda j, q=q: (0, j * strips + q)) for q in range(strips)],
        out_specs=pl.BlockSpec((m, tn), lambda j: (0, j)),
        out_shape=jax.ShapeDtypeStruct((m, n), F32),
        compiler_params=_cparams(("arbitrary",)),
        name="ada_mod",
    )(c, b.reshape(1, n), *([w] * strips))


def _inproj_kernel(x_ref, sc_ref, sh_ref, gn_ref, w_ref, gq_ref, gk_ref, gv_ref,
                   q_ref, k_ref, v_ref, u_ref, vg_ref, h_scr, *, n_heads):
    tm = x_ref.shape[0]
    y = _rms(x_ref[...], gn_ref[...])
    h = y * (1.0 + _tile_rows(sc_ref[...], tm)) + _tile_rows(sh_ref[...], tm)
    h_scr[...] = h.astype(BF16)

    tn = n_heads * HEAD_DIM
    cw = 2 * HEAD_DIM
    for part, out_ref in enumerate((q_ref, k_ref, v_ref, u_ref, vg_ref)):
        for c0 in range(0, tn, cw):
            acc = jnp.dot(h_scr[...], w_ref[:, part * tn + c0:part * tn + c0 + cw], preferred_element_type=F32)
            for hh in range(cw // HEAD_DIM):
                sl = slice(c0 + hh * HEAD_DIM, c0 + (hh + 1) * HEAD_DIM)
                a = acc[:, hh * HEAD_DIM:(hh + 1) * HEAD_DIM]
                if part == 0:
                    a = _rms(a, gq_ref[...])
                elif part == 1:
                    a = _rms(a, gk_ref[...])
                elif part == 3:
                    a = jax.nn.gelu(a)
                elif part == 4:
                    a = _rms(jax.nn.gelu(a), gv_ref[:, sl])
                out_ref[:, sl] = a.astype(out_ref.dtype)


def _inproj(x, mod, g_norm, w_in_bf, g_q, g_k, g_v, *, tm, tiles_per_batch, q_dtype, vg_dtype, col_slabs=False):
    d = g_norm.shape[0]
    n_tot = w_in_bf.shape[1]
    tn = n_tot // 5
    n_heads = tn // HEAD_DIM
    r = min(mod.shape[1], tm)
    if col_slabs:
        n_tiles = x.shape[1] // d
        tile_at = lambda i: (0, i)
        out_dims = (tm, n_tiles * tn)
    else:
        n_tiles = x.shape[0] // tm
        tile_at = lambda i: (i, 0)
        out_dims = (x.shape[0], tn)
    modspec = lambda m: pl.BlockSpec((None, r, d), lambda i, m=m: (i // tiles_per_batch, 0, m))
    rowspec = pl.BlockSpec((tm, tn), tile_at)
    vec = lambda n: pl.BlockSpec((1, n), lambda i: (0, 0))
    outs = pl.pallas_call(
        functools.partial(_inproj_kernel, n_heads=n_heads),
        grid=(n_tiles,),
        in_specs=[pl.BlockSpec((tm, d), tile_at),
                  modspec(1), modspec(0), vec(d),
                  pl.BlockSpec((d, n_tot), lambda i: (0, 0), pipeline_mode=pl.Buffered(1)),
                  vec(HEAD_DIM), vec(HEAD_DIM), vec(tn)],
        out_specs=[rowspec] * 5,
        out_shape=[jax.ShapeDtypeStruct(out_dims, dt) for dt in (q_dtype, F32, F32, F32, vg_dtype)],
        scratch_shapes=[pltpu.VMEM((tm, d), BF16)],
        compiler_params=_cparams(("arbitrary",)),
        name="in_proj",
    )(x, mod, mod, g_norm.reshape(1, d), w_in_bf, g_q.reshape(1, -1), g_k.reshape(1, -1), g_v.reshape(1, -1))
    return outs


def _moba_prompt_kernel(slopes_ref, q_ref, k_ref, v_ref, o_ref, kb_scr, vb_scr, km_scr, al_scr,
                        qa_scr, raw_scr, p_scr, m_scr, l_scr, a_scr, acc_scr, *, n_blocks, heads_per_step):
    hg = pl.program_id(1)
    qi = pl.program_id(2)
    blk = MOBA_BLOCK
    tile = 2 * blk
    c1 = HEAD_DIM ** -0.5 * LOG2E
    heads = range(heads_per_step)
    nslope = [-slopes_ref[hg * heads_per_step + e] * LOG2E for e in heads]

    @pl.when(qi == 0)
    def _():
        km_scr[...] = jnp.zeros(km_scr.shape, F32)
        lane = lax.broadcasted_iota(jnp.int32, (blk, LANES), 1)
        rel = (lax.broadcasted_iota(jnp.int32, (tile, tile), 0)
               - lax.broadcasted_iota(jnp.int32, (tile, tile), 1)).astype(F32)
        for e in heads:
            cols = slice(e * HEAD_DIM, (e + 1) * HEAD_DIM)
            for n in range(n_blocks):
                rows = slice(n * blk, (n + 1) * blk)
                kblk = k_ref[rows, cols]
                kb_scr[e, rows, 0:HEAD_DIM] = kblk.astype(BF16)
                kb_scr[e, rows, HEAD_DIM:] = jnp.where(lane == n, 1.0, 0.0).astype(BF16)
                vb_scr[e, rows, :] = v_ref[rows, cols].astype(BF16)
                km_scr[e, n:n + 1, :] = jnp.mean(kblk, axis=0, keepdims=True)
            al_scr[e] = nslope[e] * rel

    def augmented_query(e):
        q = q_ref[:, e * HEAD_DIM:(e + 1) * HEAD_DIM]
        g = lax.dot_general(km_scr[e].astype(BF16), q, _NT, preferred_element_type=F32)[:n_blocks]
        bidx = lax.broadcasted_iota(jnp.int32, g.shape, 0)
        own = 2 * qi + lax.broadcasted_iota(jnp.int32, g.shape, 1) // blk
        sel = _topk_select(g, bidx < own, n_blocks, axis=0)
        bias = jnp.where(sel | (bidx == own), 0.0, NEG).astype(F32)
        bias = jnp.concatenate([bias, jnp.full((LANES - n_blocks, tile), NEG, F32)], axis=0)
        return jnp.concatenate([q, bias.T.astype(BF16)], axis=1)

    for e in heads:
        qa_scr[e] = augmented_query(e)

    rc = SCORE_ROWS
    coli = lax.broadcasted_iota(jnp.int32, (rc, tile), 1)
    rowi = lax.broadcasted_iota(jnp.int32, (rc, tile), 0)

    def score(e, t):
        start = pl.multiple_of(t * tile, tile)
        raw_scr[e] = lax.dot_general(qa_scr[e], kb_scr[e, pl.ds(start, tile), :], _NT,
                                     preferred_element_type=F32)

    def fold(e, t, diagonal):
        start = pl.multiple_of(t * tile, tile)
        off = nslope[e] * ((qi - t) * tile).astype(F32)
        for c in range(tile // rc):
            rows = slice(c * rc, (c + 1) * rc)
            s = raw_scr[e, rows, :] * c1 + al_scr[e, rows, :]
            if diagonal:
                s = jnp.where(coli <= rowi + c * rc, s, NEG)
                m_new = jnp.broadcast_to(jnp.max(s, axis=1, keepdims=True) + off, (rc, LANES))
            else:
                m_old = m_scr[e, rows, :]
                m_new = jnp.maximum(m_old, jnp.max(s, axis=1, keepdims=True) + off)
                alpha = jnp.exp2(m_old - m_new)
                a_scr[e, rows, :] = alpha
            p = jnp.exp2(s - jnp.concatenate([m_new - off] * (tile // LANES), axis=1))
            psum = jnp.sum(p, axis=1, keepdims=True)
            p_scr[e, rows, :] = p.astype(BF16)
            m_scr[e, rows, :] = m_new
            if diagonal:
                l_scr[e, rows, :] = jnp.broadcast_to(psum, (rc, LANES))
            else:
                l_scr[e, rows, :] = alpha * l_scr[e, rows, :] + psum
        pv = jnp.dot(p_scr[e], vb_scr[e, pl.ds(start, tile), :], preferred_element_type=F32)
        if diagonal:
            acc_scr[e] = pv
        else:
            acc_scr[e] = a_scr[e] * acc_scr[e] + pv

    def visit(t, diagonal):
        for e in heads:
            score(e, t)
        for e in heads:
            fold(e, t, diagonal)

    visit(qi, True)

    def body(t, carry):
        visit(t, False)
        return carry

    lax.fori_loop(0, qi, body, 0)
    for e in heads:
        o_ref[:, e * HEAD_DIM:(e + 1) * HEAD_DIM] = (acc_scr[e] / l_scr[e]).astype(o_ref.dtype)


def _moba_prompt(q, k, v, slopes, heads_per_step=2):
    b, s, da = k.shape
    n_heads = da // HEAD_DIM
    n_blocks = s // MOBA_BLOCK
    hw = heads_per_step * HEAD_DIM
    assert s % (2 * MOBA_BLOCK) == 0 and n_blocks >= MOBA_TOPK and n_blocks <= LANES
    assert n_heads % heads_per_step == 0
    tile = 2 * MOBA_BLOCK
    qspec = pl.BlockSpec((None, tile, hw), lambda bb, hh, qi: (bb, qi, hh))
    kvspec = pl.BlockSpec((None, s, hw), lambda bb, hh, qi: (bb, 0, hh))
    return pl.pallas_call(
        functools.partial(_moba_prompt_kernel, n_blocks=n_blocks, heads_per_step=heads_per_step),
        grid=(b, n_heads // heads_per_step, s // tile),
        in_specs=[pl.BlockSpec(memory_space=pltpu.SMEM), qspec, kvspec, kvspec],
        out_specs=qspec,
        out_shape=jax.ShapeDtypeStruct((b, s, da), F32),
        scratch_shapes=[pltpu.VMEM((heads_per_step, s, 2 * HEAD_DIM), BF16),
                        pltpu.VMEM((heads_per_step, s, HEAD_DIM), BF16),
                        pltpu.VMEM((heads_per_step, LANES, HEAD_DIM), F32),
                        pltpu.VMEM((heads_per_step, tile, tile), F32),
                        pltpu.VMEM((heads_per_step, tile, 2 * HEAD_DIM), BF16),
                        pltpu.VMEM((heads_per_step, tile, tile), F32),
                        pltpu.VMEM((heads_per_step, tile, tile), BF16),
                        pltpu.VMEM((heads_per_step, tile, LANES), F32),
                        pltpu.VMEM((heads_per_step, tile, LANES), F32),
                        pltpu.VMEM((heads_per_step, tile, LANES), F32),
                        pltpu.VMEM((heads_per_step, tile, HEAD_DIM), F32)],
        compiler_params=_cparams(("arbitrary", "arbitrary", "arbitrary")),
        name="moba_prompt",
    )(slopes, q, k, v)


def _moba_sample_kernel(pt_ref, q_ref, kn_ref, vn_ref, slope_ref, *refs, n_heads, n_q, past_len, n_blocks):
    del pt_ref
    n_pages = 2 * n_blocks
    k_refs, v_refs, o_ref = refs[:n_pages], refs[n_pages:2 * n_pages], refs[2 * n_pages]
    nrow = n_q * n_heads
    scale = HEAD_DIM ** -0.5
    page = k_refs[0].shape[0]
    pkeys = page * n_heads
    lane128 = lax.broadcasted_iota(jnp.int32, (nrow, LANES), 1)
    rowc = lax.broadcasted_iota(jnp.int32, (nrow, 1), 0)
    tq = past_len + rowc // n_heads
    slope = slope_ref[...]

    seq_row = pl.ds(pl.program_id(0) % SUBLANES, 1)

    def rows_of(ref):
        flat = ref[seq_row, :]
        return jnp.concatenate([flat[:, j * HEAD_DIM:(j + 1) * HEAD_DIM] for j in range(nrow)], axis=0)

    qb = rows_of(q_ref).astype(BF16)
    qf = qb.astype(F32)

    def masked_scores(raw, pos0, width, extra_ok=None):
        lane = lax.broadcasted_iota(jnp.int32, (nrow, width), 1)
        kpos = pos0 + lane // n_heads
        s = raw * scale - slope * (tq - kpos).astype(F32)
        ok = (lane % n_heads) == (rowc % n_heads)
        if extra_ok is not None:
            ok = ok & extra_ok(lane, kpos)
        return jnp.where(ok, s, NEG)

    g_all = jnp.zeros((nrow, LANES), F32)
    m_all = jnp.zeros((nrow, LANES), F32)
    l_all = jnp.zeros((nrow, LANES), F32)
    zq = jnp.zeros_like(qb)
    q_pair = jnp.concatenate([jnp.concatenate([qb, zq], axis=1), jnp.concatenate([zq, qb], axis=1)], axis=0)
    as_keys = lambda ref: ref[...].reshape(pkeys, HEAD_DIM).astype(BF16)
    raws = []
    for n in range(n_blocks):
        k0 = k_refs[2 * n][...]
        k1 = k_refs[2 * n + 1][...]
        kk = jnp.concatenate([k0.reshape(pkeys, HEAD_DIM).astype(BF16),
                              k1.reshape(pkeys, HEAD_DIM).astype(BF16)], axis=1)
        raws.append(lax.dot_general(q_pair, kk, _NT, preferred_element_type=F32))
        kmean = ((jnp.sum(k0, axis=0) + jnp.sum(k1, axis=0)) / MOBA_BLOCK).astype(BF16).astype(F32)
        g_n = jnp.sum(qf * jnp.concatenate([kmean] * n_q, axis=0), axis=1, keepdims=True)
        g_all = jnp.where(lane128 == n, g_n, g_all)
    pps = []
    for n in range(n_blocks):
        s0 = masked_scores(raws[n][:nrow], n * MOBA_BLOCK, pkeys)
        s1 = masked_scores(raws[n][nrow:], n * MOBA_BLOCK + page, pkeys)
        m_n = jnp.maximum(jnp.max(s0, axis=1, keepdims=True), jnp.max(s1, axis=1, keepdims=True))
        p0 = jnp.exp(s0 - m_n)
        p1 = jnp.exp(s1 - m_n)
        l_n = jnp.sum(p0, axis=1, keepdims=True) + jnp.sum(p1, axis=1, keepdims=True)
        pps.append(jnp.concatenate([p0, p1], axis=0).astype(BF16))
        m_all = jnp.where(lane128 == n, m_n, m_all)
        l_all = jnp.where(lane128 == n, l_n, l_all)
    outs = []
    for n in range(n_blocks):
        vv = jnp.concatenate([as_keys(v_refs[2 * n]), as_keys(v_refs[2 * n + 1])], axis=1)
        o2 = jnp.dot(pps[n], vv, preferred_element_type=F32)
        outs.append(o2[:nrow, :HEAD_DIM] + o2[nrow:, HEAD_DIM:])

    sel = _topk_select(g_all, lane128 < n_blocks, n_blocks, axis=1)
    pad = jnp.zeros((LANES - nrow, HEAD_DIM), F32)
    own_k = jnp.concatenate([rows_of(kn_ref), pad], axis=0).astype(BF16)
    own_v = jnp.concatenate([rows_of(vn_ref), pad], axis=0).astype(BF16)
    s_own = masked_scores(lax.dot_general(qb, own_k, _NT, preferred_element_type=F32), past_len, LANES,
                          lambda lane, kpos: (lane < nrow) & (kpos <= tq))
    m_fin = jnp.maximum(jnp.max(jnp.where(sel, m_all, NEG), axis=1, keepdims=True),
                        jnp.max(s_own, axis=1, keepdims=True))
    w = jnp.where(sel, jnp.exp(jnp.where(sel, m_all - m_fin, 0.0)), 0.0)
    p_own = jnp.exp(s_own - m_fin)
    l_fin = jnp.sum(w * l_all, axis=1, keepdims=True) + jnp.sum(p_own, axis=1, keepdims=True)
    comb = jnp.dot(p_own.astype(BF16), own_v, preferred_element_type=F32)
    for n in range(n_blocks):
        comb = comb + w[:, n:n + 1] * outs[n]
    out = (comb / l_fin).astype(o_ref.dtype)
    o_ref[seq_row, :] = jnp.concatenate([out[j:j + 1, :] for j in range(nrow)], axis=1)


def _moba_sample(q, k_new, v_new, cache_k, cache_v, layer, page_table, slopes_np, past_len):
    db, width = q.shape
    page, n_heads, hd = cache_k.shape[2:]
    nrow = width // hd
    n_q = nrow // n_heads
    n_pages = page_table.shape[1]
    n_blocks = past_len // MOBA_BLOCK
    assert past_len % MOBA_BLOCK == 0 and MOBA_BLOCK == 2 * page and n_pages * page == past_len
    assert n_blocks >= MOBA_TOPK and n_blocks <= LANES and n_heads == SUBLANES
    assert nrow % 16 == 0 and nrow <= LANES and db % SUBLANES == 0
    slope_col = np.tile(slopes_np, n_q).reshape(nrow, 1)
    seqspec = pl.BlockSpec((SUBLANES, width), lambda s, pt: (s // SUBLANES, 0))
    pagespecs = [pl.BlockSpec((None, None, page, n_heads, hd),
                              lambda s, pt, j=j: (layer, pt[s * n_pages + j], 0, 0, 0))
                 for j in range(n_pages)]
    grid_spec = pltpu.PrefetchScalarGridSpec(
        num_scalar_prefetch=1,
        grid=(db,),
        in_specs=[seqspec, seqspec, seqspec, pl.BlockSpec((nrow, 1), lambda s, pt: (0, 0))] + pagespecs * 2,
        out_specs=seqspec,
    )
    return pl.pallas_call(
        functools.partial(_moba_sample_kernel, n_heads=n_heads, n_q=n_q, past_len=past_len, n_blocks=n_blocks),
        grid_spec=grid_spec,
        out_shape=jax.ShapeDtypeStruct((db, width), F32),
        compiler_params=_cparams(("arbitrary",)),
        name="moba_sample",
    )(page_table.reshape(-1), q, k_new, v_new, jnp.asarray(slope_col),
      *([cache_k] * n_pages), *([cache_v] * n_pages))


def _outproj_kernel(oa_ref, u_ref, vg_ref, x_ref, gt_ref, sc_ref, sh_ref, ga_ref, gg_ref, gf_ref,
                    wo_ref, ws_ref, bs_ref, x1_ref, h2_ref, og_scr, *, col_slabs, n_groups):
    tm = x_ref.shape[0]
    da = n_groups * GROUP_DIM
    d = gf_ref.shape[1]
    if col_slabs:
        n_t = ws_ref.shape[0]
        for t in range(n_t):
            mixed = jnp.zeros((tm, da), F32) + bs_ref[t:t + 1, :]
            for s in range(t + 1):
                mixed = mixed + ws_ref[t, s:s + 1, :] * vg_ref[:, s * da:(s + 1) * da].astype(F32)
            og_scr[:, t * da:(t + 1) * da] = u_ref[:, t * da:(t + 1) * da] * mixed
        parts = [(slice(None), t) for t in range(n_t)]
    else:
        pm = min(tm, 256)
        parts = [(slice(r * pm, (r + 1) * pm), 0) for r in range(tm // pm)]
        tri = (lax.broadcasted_iota(jnp.int32, (CHUNK, CHUNK), 0)
               >= lax.broadcasted_iota(jnp.int32, (CHUNK, CHUNK), 1))
        for gi in range(n_groups):
            sl = slice(gi * GROUP_DIM, (gi + 1) * GROUP_DIM)
            w = jnp.where(tri, ws_ref[gi], 0.0).astype(BF16)
            for c in range(tm // CHUNK):
                rs = slice(c * CHUNK, (c + 1) * CHUNK)
                mixed = jnp.dot(w, vg_ref[rs, sl].astype(BF16), preferred_element_type=F32) + bs_ref[:, sl]
                og_scr[rs, sl] = u_ref[rs, sl] * mixed
    gt, sc, sh = gt_ref[...], sc_ref[...], sh_ref[...]
    for rows, t in parts:
        ca, cd = slice(t * da, (t + 1) * da), slice(t * d, (t + 1) * d)
        a_n = _rms(oa_ref[rows, ca], ga_ref[...]).astype(BF16)
        g_n = _rms(og_scr[rows, ca], gg_ref[...]).astype(BF16)
        out = (jnp.dot(a_n, wo_ref[0:da, :], preferred_element_type=F32)
               + jnp.dot(g_n, wo_ref[da:, :], preferred_element_type=F32))
        x1 = x_ref[rows, cd] + gt * out
        x1_ref[rows, cd] = x1
        h2 = _rms(x1, gf_ref[...]) * (1.0 + sc) + sh
        h2_ref[rows, cd] = h2.astype(h2_ref.dtype)


def _outproj(oa, u, vg, x, mod, g_oa, g_og, g_ffn, w_out_bf, ws, bs, *, tm, tiles_per_batch, col_slabs):
    rows = x.shape[0]
    d = g_ffn.shape[0]
    da = g_oa.shape[0]
    r = min(mod.shape[1], tm)
    modspec = lambda m: pl.BlockSpec((None, r, d), lambda i, m=m: (i // tiles_per_batch, 0, m))
    row = lambda a: pl.BlockSpec((tm, a.shape[1]), lambda i: (i, 0))
    full = lambda a: pl.BlockSpec(a.shape, lambda i, nd=a.ndim: (0,) * nd, pipeline_mode=pl.Buffered(1))
    g_oa, g_og, g_ffn = g_oa.reshape(1, -1), g_og.reshape(1, -1), g_ffn.reshape(1, -1)
    return pl.pallas_call(
        functools.partial(_outproj_kernel, col_slabs=col_slabs, n_groups=da // GROUP_DIM),
        grid=(rows // tm,),
        in_specs=[row(oa), row(u), row(vg), row(x), modspec(2), modspec(4), modspec(3),
                  full(g_oa), full(g_og), full(g_ffn), full(w_out_bf), full(ws), full(bs)],
        out_specs=[row(x), row(x)],
        out_shape=[jax.ShapeDtypeStruct(x.shape, F32), jax.ShapeDtypeStruct(x.shape, BF16)],
        scratch_shapes=[pltpu.VMEM((tm, oa.shape[1]), F32)],
        compiler_params=_cparams(("arbitrary",)),
        name="gmlp_out_proj",
    )(oa, u, vg, x, mod, mod, mod, g_oa, g_og, g_ffn, w_out_bf, ws, bs)


def _ffn_kernel(h_ref, x1_ref, gt_ref, wa_ref, wp_ref, wc_ref, bc_ref, wd_ref, prev_ref,
                y_ref, conv_ref, abuf, halo, *stack, shift, tiles_per_batch, n_parts, n_slabs):
    i = pl.program_id(0)
    f = pl.program_id(1)
    nf = pl.num_programs(1)
    d = wd_ref.shape[1]
    keep = (CONV_W - 1) * shift
    off = -(-keep // SUBLANES) * SUBLANES
    tm = abuf.shape[0] - off
    if n_slabs > 1:
        h_src, = stack

        @pl.when(f == 0)
        def _():
            for t in range(n_slabs):
                h_src[t * shift:(t + 1) * shift, :] = h_ref[:, t * d:(t + 1) * d]
    else:
        h_src = h_ref

    @pl.when(i % tiles_per_batch == 0)
    def _():
        abuf[off - keep:off, :] = prev_ref[...]

    @pl.when(i % tiles_per_batch != 0)
    def _():
        abuf[off - keep:off, :] = halo[f]

    @pl.when(f == 0)
    def _():
        y_ref[...] = jnp.zeros(y_ref.shape, F32)

    def accumulate(row0, val):
        if n_slabs == 1:
            y_ref[row0:row0 + val.shape[0], :] += val
        else:
            for k in range(val.shape[0] // shift):
                t = row0 // shift + k
                y_ref[:, t * d:(t + 1) * d] += val[k * shift:(k + 1) * shift, :]

    hm = tm // n_parts
    a_parts, p_parts = [], []
    for r in range(n_parts):
        rows = slice(r * hm, (r + 1) * hm)
        h = h_src[rows, :]
        a = jnp.dot(h, wa_ref[...], preferred_element_type=F32)
        abuf[off + r * hm:off + (r + 1) * hm, :] = a
        a_parts.append(a)
        p_parts.append(jnp.dot(h, wp_ref[...], preferred_element_type=F32))
    for r in range(n_parts):
        rows = slice(r * hm, (r + 1) * hm)
        ac = bc_ref[...] + wc_ref[CONV_W - 1:CONV_W, :] * a_parts[r]
        for jj in range(CONV_W - 1):
            start = off + r * hm - (CONV_W - 1 - jj) * shift
            ac = ac + wc_ref[jj:jj + 1, :] * abuf[start:start + hm, :]
        gate = (ac * jax.nn.sigmoid(ac)) * p_parts[r]
        accumulate(r * hm, jnp.dot(gate.astype(BF16), wd_ref[...], preferred_element_type=F32))
    last = abuf[off + tm - keep:off + tm, :]
    halo[f] = last
    conv_ref[...] = last

    @pl.when(f == nf - 1)
    def _():
        gt = gt_ref[...]
        for t in range(n_slabs):
            cols = slice(t * d, (t + 1) * d)
            y_ref[:, cols] = x1_ref[:, cols] + gt * y_ref[:, cols]


def _ffn(h2, x1, mod, w_up_bf, w_conv, b_conv, w_down_bf, conv_prev, *, tm, tf, tiles_per_batch, shift,
         n_slabs=1):
    d = w_down_bf.shape[1]
    ff = w_down_bf.shape[0]
    nf = ff // tf
    keep = conv_prev.shape[1]
    off = -(-keep // SUBLANES) * SUBLANES
    n_tiles = 1 if n_slabs > 1 else x1.shape[0] // tm
    blk_rows = tm // n_slabs
    r = min(mod.shape[1], blk_rows)
    stack = [pltpu.VMEM((tm, d), BF16)] if n_slabs > 1 else []
    y, tail = pl.pallas_call(
        functools.partial(_ffn_kernel, shift=shift, tiles_per_batch=tiles_per_batch,
                          n_parts=tm // FFN_PART_ROWS, n_slabs=n_slabs),
        grid=(n_tiles, nf),
        in_specs=[pl.BlockSpec((blk_rows, n_slabs * d), lambda i, f: (i, 0)),
                  pl.BlockSpec((blk_rows, n_slabs * d), lambda i, f: (i, 0)),
                  pl.BlockSpec((None, r, d), lambda i, f: (i // tiles_per_batch, 0, 5)),
                  pl.BlockSpec((d, tf), lambda i, f: (0, f)),
                  pl.BlockSpec((d, tf), lambda i, f: (0, f + nf)),
                  pl.BlockSpec((CONV_W, tf), lambda i, f: (0, f)),
                  pl.BlockSpec((1, tf), lambda i, f: (0, f)),
                  pl.BlockSpec((tf, d), lambda i, f: (f, 0)),
                  pl.BlockSpec((None, keep, tf), lambda i, f: (i // tiles_per_batch, 0, f))],
        out_specs=[pl.BlockSpec((blk_rows, n_slabs * d), lambda i, f: (i, 0)),
                   pl.BlockSpec((None, keep, tf), lambda i, f: (i, 0, f))],
        out_shape=[jax.ShapeDtypeStruct(x1.shape, F32), jax.ShapeDtypeStruct((n_tiles, keep, ff), F32)],
        scratch_shapes=[pltpu.VMEM((off + tm, tf), F32), pltpu.VMEM((nf, keep, tf), F32)] + stack,
        compiler_params=_cparams(("arbitrary", "arbitrary")),
        name="conv_ffn",
    )(h2, x1, mod, w_up_bf, w_up_bf, w_conv, b_conv.reshape(1, ff), w_down_bf, conv_prev)
    return y, tail[tiles_per_batch - 1::tiles_per_batch]


def kernel(x_prompt, x_sample, cache_k, cache_v, state_conv, page_table, c_prompt, c_sample, w_ada, b_ada, g_norm_mix, w_in, g_q, g_k, g_v, w_s, b_s, g_out_attn, g_out_gmlp, w_out, g_norm_ffn, w_up, w_conv, b_conv, w_down):
    depth = w_ada.shape[0]
    b, s, d = x_prompt.shape
    db, n_q, _ = x_sample.shape
    page, n_heads, hd = cache_k.shape[2:]
    da = n_heads * hd
    ff = w_down.shape[1]
    n_groups = w_s.shape[1]
    past_len = page_table.shape[1] * page
    assert hd == HEAD_DIM and da == n_groups * GROUP_DIM and s % CHUNK == 0 and n_q <= CHUNK
    assert w_in.shape[2] == 5 * da
    slopes_np = (2.0 ** (-8.0 * np.arange(1, n_heads + 1) / n_heads)).astype(np.float32)
    slopes = jnp.asarray(slopes_np)

    n_c = b + db
    pad_c = -(-n_c // SUBLANES) * SUBLANES
    c_all = jnp.concatenate([c_sample, c_prompt, jnp.zeros((pad_c - n_c, d), F32)], axis=0)

    h_p = x_prompt.reshape(b * s, d)
    h_s = x_sample.reshape(db, n_q * d)
    tm_p = 512
    tm_ffn = 512
    outs = [[] for _ in range(7)]
    for layer in range(depth):
        w_in_bf = w_in[layer].astype(BF16)
        w_out_bf = w_out[layer].astype(BF16)
        w_up_bf = w_up[layer].astype(BF16)
        w_down_bf = w_down[layer].astype(BF16)

        mod = _ada(c_all, w_ada[layer], b_ada[layer])
        mod_p = mod[db:n_c].reshape(b, 1, N_MOD * d)
        mod_s = mod.reshape(1, pad_c, N_MOD * d)

        q, k, v, u, vg = _inproj(h_p, mod_p, g_norm_mix[layer], w_in_bf, g_q[layer], g_k[layer], g_v[layer],
                                 tm=tm_p, tiles_per_batch=s // tm_p, q_dtype=BF16, vg_dtype=BF16)
        o_attn = _moba_prompt(q.reshape(b, s, da), k.reshape(b, s, da), v.reshape(b, s, da), slopes)
        bs_rows = jnp.repeat(b_s[layer].T, GROUP_DIM, axis=1)
        x1, h2 = _outproj(o_attn.reshape(b * s, da), u, vg, h_p, mod_p, g_out_attn[layer], g_out_gmlp[layer],
                          g_norm_ffn[layer], w_out_bf, w_s[layer], bs_rows,
                          tm=tm_p, tiles_per_batch=s // tm_p, col_slabs=False)
        conv0 = jnp.zeros((b, CONV_W - 1, ff), F32)
        y_p, conv_p = _ffn(h2, x1, mod_p, w_up_bf, w_conv[layer], b_conv[layer], w_down_bf, conv0,
                           tm=tm_ffn, tf=512, tiles_per_batch=s // tm_ffn, shift=1)

        qs, ks, vs, us, vgs = _inproj(h_s, mod_s, g_norm_mix[layer], w_in_bf, g_q[layer], g_k[layer], g_v[layer],
                                      tm=db, tiles_per_batch=n_q, q_dtype=F32, vg_dtype=F32, col_slabs=True)
        o_s = _moba_sample(qs, ks, vs, cache_k, cache_v, layer, page_table, slopes_np, past_len)
        ws_t = jnp.repeat(jnp.tril(w_s[layer][:, :n_q, :n_q]).transpose(1, 2, 0), GROUP_DIM, axis=2)
        bs_t = jnp.repeat(b_s[layer][:, :n_q].T, GROUP_DIM, axis=1)
        x1s, h2s = _outproj(o_s, us, vgs, h_s, mod_s, g_out_attn[layer], g_out_gmlp[layer], g_norm_ffn[layer],
                            w_out_bf, ws_t, bs_t, tm=db, tiles_per_batch=1, col_slabs=True)
        prev_s = state_conv[layer].transpose(1, 0, 2).reshape(1, (CONV_W - 1) * db, ff)
        y_s, conv_s = _ffn(h2s, x1s, mod_s, w_up_bf, w_conv[layer], b_conv[layer], w_down_bf, prev_s,
                           tm=n_q * db, tf=512, tiles_per_batch=1, shift=db, n_slabs=n_q)

        h_p, h_s = y_p, y_s
        outs[0].append(k.reshape(b, s // page, page, n_heads, hd))
        outs[1].append(v.reshape(b, s // page, page, n_heads, hd))
        outs[2].append(ks.reshape(db, n_q, n_heads, hd))
        outs[3].append(vs.reshape(db, n_q, n_heads, hd))
        outs[4].append(conv_p)
        outs[5].append(conv_s.reshape(CONV_W - 1, db, ff).transpose(1, 0, 2))
        outs[6].append(vgs.reshape(db, n_q, da))

    y_prompt = h_p.reshape(b, s, d)
    y_sample = h_s.reshape(db, n_q, d)
    return (y_prompt, y_sample) + tuple(jnp.stack(o) for o in outs)
```

```python
import functools
import math

import numpy as np
import jax
import jax.numpy as jnp
from jax import lax
from jax.experimental import pallas as pl
from jax.experimental.pallas import tpu as pltpu

F32 = jnp.float32
BF16 = jnp.bfloat16

HEAD_DIM = 128
GROUP_DIM = 128
MOBA_BLOCK = 256
MOBA_TOPK = 3
CHUNK = 128
CONV_W = 3
N_MOD = 6
EPS = 1e-6
NEG = -1e30
LOG2E = math.log2(math.e)
LANES = 128
SUBLANES = 8
FFN_PART_ROWS = 256
SCORE_ROWS = 64
VMEM_LIMIT = 56 * 1024 * 1024

_NT = (((1,), (1,)), ((), ()))


def _cparams(sem):
    return pltpu.CompilerParams(dimension_semantics=sem, vmem_limit_bytes=VMEM_LIMIT)


def _rms(x, g):
    ms = jnp.mean(x * x, axis=-1, keepdims=True)
    return (x * lax.rsqrt(ms + EPS)) * g


def _tile_rows(m, rows):
    r = m.shape[0]
    if r == 1 or r == rows:
        return m
    return jnp.concatenate([m] * (rows // r), axis=0)


def _topk_select(g, valid, n_blocks, axis):
    blk = lax.broadcasted_iota(jnp.int32, g.shape, axis)
    gm = jnp.where(valid, g, -jnp.inf)
    rank = jnp.zeros(g.shape, jnp.int32)
    for m in range(n_blocks):
        one = gm[m:m + 1, :] if axis == 0 else gm[:, m:m + 1]
        beats = (one > gm) | ((one == gm) & (m < blk))
        rank = rank + beats.astype(jnp.int32)
    return valid & (rank < MOBA_TOPK)


def _ada_kernel(c_ref, b_ref, *refs):
    w_refs, o_ref = refs[:-1], refs[-1]
    c = c_ref[...]
    s = (c * jax.nn.sigmoid(c)).astype(BF16)
    tw = w_refs[0].shape[1]
    for q, w_ref in enumerate(w_refs):
        cols = slice(q * tw, (q + 1) * tw)
        o_ref[:, cols] = jnp.dot(s, w_ref[...].astype(BF16), preferred_element_type=F32) + b_ref[:, cols]


def _ada(c, w, b):
    m, d = c.shape
    n = w.shape[1]
    tn, strips = 2048, 4
    tw = tn // strips
    return pl.pallas_call(
        _ada_kernel,
        grid=(n // tn,),
        in_specs=[pl.BlockSpec((m, d), lambda j: (0, 0)),
                  pl.BlockSpec((1, tn), lambda j: (0, j))]
                 + [pl.BlockSpec((d, tw), lambda j, q=q: (0, j * strips + q)) for q in range(strips)],
        out_specs=pl.BlockSpec((m, tn), lambda j: (0, j)),
        out_shape=jax.ShapeDtypeStruct((m, n), F32),
        compiler_params=_cparams(("arbitrary",)),
        name="ada_mod",
    )(c, b.reshape(1, n), *([w] * strips))


def _inproj_kernel(x_ref, sc_ref, sh_ref, gn_ref, w_ref, gq_ref, gk_ref, gv_ref,
                   q_ref, k_ref, v_ref, u_ref, vg_ref, h_scr, *, n_heads):
    tm = x_ref.shape[0]
    y = _rms(x_ref[...], gn_ref[...])
    h = y * (1.0 + _tile_rows(sc_ref[...], tm)) + _tile_rows(sh_ref[...], tm)
    h_scr[...] = h.astype(BF16)

    tn = n_heads * HEAD_DIM
    cw = 2 * HEAD_DIM
    for part, out_ref in enumerate((q_ref, k_ref, v_ref, u_ref, vg_ref)):
        for c0 in range(0, tn, cw):
            acc = jnp.dot(h_scr[...], w_ref[:, part * tn + c0:part * tn + c0 + cw], preferred_element_type=F32)
            for hh in range(cw // HEAD_DIM):
                sl = slice(c0 + hh * HEAD_DIM, c0 + (hh + 1) * HEAD_DIM)
                a = acc[:, hh * HEAD_DIM:(hh + 1) * HEAD_DIM]
                if part == 0:
                    a = _rms(a, gq_ref[...])
                elif part == 1:
                    a = _rms(a, gk_ref[...])
                elif part == 3:
                    a = jax.nn.gelu(a)
                elif part == 4:
                    a = _rms(jax.nn.gelu(a), gv_ref[:, sl])
                out_ref[:, sl] = a.astype(out_ref.dtype)


def _inproj(x, mod, g_norm, w_in_bf, g_q, g_k, g_v, *, tm, tiles_per_batch, q_dtype, vg_dtype, col_slabs=False):
    d = g_norm.shape[0]
    n_tot = w_in_bf.shape[1]
    tn = n_tot // 5
    n_heads = tn // HEAD_DIM
    r = min(mod.shape[1], tm)
    if col_slabs:
        n_tiles = x.shape[1] // d
        tile_at = lambda i: (0, i)
        out_dims = (tm, n_tiles * tn)
    else:
        n_tiles = x.shape[0] // tm
        tile_at = lambda i: (i, 0)
        out_dims = (x.shape[0], tn)
    modspec = lambda m: pl.BlockSpec((None, r, d), lambda i, m=m: (i // tiles_per_batch, 0, m))
    rowspec = pl.BlockSpec((tm, tn), tile_at)
    vec = lambda n: pl.BlockSpec((1, n), lambda i: (0, 0))
    outs = pl.pallas_call(
        functools.partial(_inproj_kernel, n_heads=n_heads),
        grid=(n_tiles,),
        in_specs=[pl.BlockSpec((tm, d), tile_at),
                  modspec(1), modspec(0), vec(d),
                  pl.BlockSpec((d, n_tot), lambda i: (0, 0), pipeline_mode=pl.Buffered(1)),
                  vec(HEAD_DIM), vec(HEAD_DIM), vec(tn)],
        out_specs=[rowspec] * 5,
        out_shape=[jax.ShapeDtypeStruct(out_dims, dt) for dt in (q_dtype, F32, F32, F32, vg_dtype)],
        scratch_shapes=[pltpu.VMEM((tm, d), BF16)],
        compiler_params=_cparams(("arbitrary",)),
        name="in_proj",
    )(x, mod, mod, g_norm.reshape(1, d), w_in_bf, g_q.reshape(1, -1), g_k.reshape(1, -1), g_v.reshape(1, -1))
    return outs


def _moba_prompt_kernel(slopes_ref, q_ref, k_ref, v_ref, o_ref, kb_scr, vb_scr, km_scr, al_scr,
                        qa_scr, raw_scr, p_scr, m_scr, l_scr, a_scr, acc_scr, *, n_blocks, heads_per_step):
    hg = pl.program_id(1)
    qi = pl.program_id(2)
    blk = MOBA_BLOCK
    tile = 2 * blk
    c1 = HEAD_DIM ** -0.5 * LOG2E
    heads = range(heads_per_step)
    nslope = [-slopes_ref[hg * heads_per_step + e] * LOG2E for e in heads]

    @pl.when(qi == 0)
    def _():
        km_scr[...] = jnp.zeros(km_scr.shape, F32)
        lane = lax.broadcasted_iota(jnp.int32, (blk, LANES), 1)
        rel = (lax.broadcasted_iota(jnp.int32, (tile, tile), 0)
               - lax.broadcasted_iota(jnp.int32, (tile, tile), 1)).astype(F32)
        for e in heads:
            cols = slice(e * HEAD_DIM, (e + 1) * HEAD_DIM)
            for n in range(n_blocks):
                rows = slice(n * blk, (n + 1) * blk)
                kblk = k_ref[rows, cols]
                kb_scr[e, rows, 0:HEAD_DIM] = kblk.astype(BF16)
                kb_scr[e, rows, HEAD_DIM:] = jnp.where(lane == n, 1.0, 0.0).astype(BF16)
                vb_scr[e, rows, :] = v_ref[rows, cols].astype(BF16)
                km_scr[e, n:n + 1, :] = jnp.mean(kblk, axis=0, keepdims=True)
            al_scr[e] = nslope[e] * rel

    def augmented_query(e):
        q = q_ref[:, e * HEAD_DIM:(e + 1) * HEAD_DIM]
        g = lax.dot_general(km_scr[e].astype(BF16), q, _NT, preferred_element_type=F32)[:n_blocks]
        bidx = lax.broadcasted_iota(jnp.int32, g.shape, 0)
        own = 2 * qi + lax.broadcasted_iota(jnp.int32, g.shape, 1) // blk
        sel = _topk_select(g, bidx < own, n_blocks, axis=0)
        bias = jnp.where(sel | (bidx == own), 0.0, NEG).astype(F32)
        bias = jnp.concatenate([bias, jnp.full((LANES - n_blocks, tile), NEG, F32)], axis=0)
        return jnp.concatenate([q, bias.T.astype(BF16)], axis=1)

    for e in heads:
        qa_scr[e] = augmented_query(e)

    rc = SCORE_ROWS
    coli = lax.broadcasted_iota(jnp.int32, (rc, tile), 1)
    rowi = lax.broadcasted_iota(jnp.int32, (rc, tile), 0)

    def score(e, t):
        start = pl.multiple_of(t * tile, tile)
        raw_scr[e] = lax.dot_general(qa_scr[e], kb_scr[e, pl.ds(start, tile), :], _NT,
                                     preferred_element_type=F32)

    def fold(e, t, diagonal):
        start = pl.multiple_of(t * tile, tile)
        off = nslope[e] * ((qi - t) * tile).astype(F32)
        for c in range(tile // rc):
            rows = slice(c * rc, (c + 1) * rc)
            s = raw_scr[e, rows, :] * c1 + al_scr[e, rows, :]
            if diagonal:
                s = jnp.where(coli <= rowi + c * rc, s, NEG)
                m_new = jnp.broadcast_to(jnp.max(s, axis=1, keepdims=True) + off, (rc, LANES))
            else:
                m_old = m_scr[e, rows, :]
                m_new = jnp.maximum(m_old, jnp.max(s, axis=1, keepdims=True) + off)
                alpha = jnp.exp2(m_old - m_new)
                a_scr[e, rows, :] = alpha
            p = jnp.exp2(s - jnp.concatenate([m_new - off] * (tile // LANES), axis=1))
            psum = jnp.sum(p, axis=1, keepdims=True)
            p_scr[e, rows, :] = p.astype(BF16)
            m_scr[e, rows, :] = m_new
            if diagonal:
                l_scr[e, rows, :] = jnp.broadcast_to(psum, (rc, LANES))
            else:
                l_scr[e, rows, :] = alpha * l_scr[e, rows, :] + psum
        pv = jnp.dot(p_scr[e], vb_scr[e, pl.ds(start, tile), :], preferred_element_type=F32)
        if diagonal:
            acc_scr[e] = pv
        else:
            acc_scr[e] = a_scr[e] * acc_scr[e] + pv

    def visit(t, diagonal):
        for e in heads:
            score(e, t)
        for e in heads:
            fold(e, t, diagonal)

    visit(qi, True)

    def body(t, carry):
        visit(t, False)
        return carry

    lax.fori_loop(0, qi, body, 0)
    for e in heads:
        o_ref[:, e * HEAD_DIM:(e + 1) * HEAD_DIM] = (acc_scr[e] / l_scr[e]).astype(o_ref.dtype)


def _moba_prompt(q, k, v, slopes, heads_per_step=2):
    b, s, da = k.shape
    n_heads = da // HEAD_DIM
    n_blocks = s // MOBA_BLOCK
    hw = heads_per_step * HEAD_DIM
    assert s % (2 * MOBA_BLOCK) == 0 and n_blocks >= MOBA_TOPK and n_blocks <= LANES
    assert n_heads % heads_per_step == 0
    tile = 2 * MOBA_BLOCK
    qspec = pl.BlockSpec((None, tile, hw), lambda bb, hh, qi: (bb, qi, hh))
    kvspec = pl.BlockSpec((None, s, hw), lambda bb, hh, qi: (bb, 0, hh))
    return pl.pallas_call(
        functools.partial(_moba_prompt_kernel, n_blocks=n_blocks, heads_per_step=heads_per_step),
        grid=(b, n_heads // heads_per_step, s // tile),
        in_specs=[pl.BlockSpec(memory_space=pltpu.SMEM), qspec, kvspec, kvspec],
        out_specs=qspec,
        out_shape=jax.ShapeDtypeStruct((b, s, da), F32),
        scratch_shapes=[pltpu.VMEM((heads_per_step, s, 2 * HEAD_DIM), BF16),
                        pltpu.VMEM((heads_per_step, s, HEAD_DIM), BF16),
                        pltpu.VMEM((heads_per_step, LANES, HEAD_DIM), F32),
                        pltpu.VMEM((heads_per_step, tile, tile), F32),
                        pltpu.VMEM((heads_per_step, tile, 2 * HEAD_DIM), BF16),
                        pltpu.VMEM((heads_per_step, tile, tile), F32),
                        pltpu.VMEM((heads_per_step, tile, tile), BF16),
                        pltpu.VMEM((heads_per_step, tile, LANES), F32),
                        pltpu.VMEM((heads_per_step, tile, LANES), F32),
                        pltpu.VMEM((heads_per_step, tile, LANES), F32),
                        pltpu.VMEM((heads_per_step, tile, HEAD_DIM), F32)],
        compiler_params=_cparams(("arbitrary", "arbitrary", "arbitrary")),
        name="moba_prompt",
    )(slopes, q, k, v)


def _moba_sample_kernel(pt_ref, q_ref, kn_ref, vn_ref, slope_ref, *refs, n_heads, n_q, past_len, n_blocks):
    del pt_ref
    n_pages = 2 * n_blocks
    k_refs, v_refs, o_ref = refs[:n_pages], refs[n_pages:2 * n_pages], refs[2 * n_pages]
    nrow = n_q * n_heads
    scale = HEAD_DIM ** -0.5
    page = k_refs[0].shape[0]
    pkeys = page * n_heads
    lane128 = lax.broadcasted_iota(jnp.int32, (nrow, LANES), 1)
    rowc = lax.broadcasted_iota(jnp.int32, (nrow, 1), 0)
    tq = past_len + rowc // n_heads
    slope = slope_ref[...]

    seq_row = pl.ds(pl.program_id(0) % SUBLANES, 1)

    def rows_of(ref):
        flat = ref[seq_row, :]
        return jnp.concatenate([flat[:, j * HEAD_DIM:(j + 1) * HEAD_DIM] for j in range(nrow)], axis=0)

    qb = rows_of(q_ref).astype(BF16)
    qf = qb.astype(F32)

    def masked_scores(raw, pos0, width, extra_ok=None):
        lane = lax.broadcasted_iota(jnp.int32, (nrow, width), 1)
        kpos = pos0 + lane // n_heads
        s = raw * scale - slope * (tq - kpos).astype(F32)
        ok = (lane % n_heads) == (rowc % n_heads)
        if extra_ok is not None:
            ok = ok & extra_ok(lane, kpos)
        return jnp.where(ok, s, NEG)

    g_all = jnp.zeros((nrow, LANES), F32)
    m_all = jnp.zeros((nrow, LANES), F32)
    l_all = jnp.zeros((nrow, LANES), F32)
    zq = jnp.zeros_like(qb)
    q_pair = jnp.concatenate([jnp.concatenate([qb, zq], axis=1), jnp.concatenate([zq, qb], axis=1)], axis=0)
    as_keys = lambda ref: ref[...].reshape(pkeys, HEAD_DIM).astype(BF16)
    raws = []
    for n in range(n_blocks):
        k0 = k_refs[2 * n][...]
        k1 = k_refs[2 * n + 1][...]
        kk = jnp.concatenate([k0.reshape(pkeys, HEAD_DIM).astype(BF16),
                              k1.reshape(pkeys, HEAD_DIM).astype(BF16)], axis=1)
        raws.append(lax.dot_general(q_pair, kk, _NT, preferred_element_type=F32))
        kmean = ((jnp.sum(k0, axis=0) + jnp.sum(k1, axis=0)) / MOBA_BLOCK).astype(BF16).astype(F32)
        g_n = jnp.sum(qf * jnp.concatenate([kmean] * n_q, axis=0), axis=1, keepdims=True)
        g_all = jnp.where(lane128 == n, g_n, g_all)
    pps = []
    for n in range(n_blocks):
        s0 = masked_scores(raws[n][:nrow], n * MOBA_BLOCK, pkeys)
        s1 = masked_scores(raws[n][nrow:], n * MOBA_BLOCK + page, pkeys)
        m_n = jnp.maximum(jnp.max(s0, axis=1, keepdims=True), jnp.max(s1, axis=1, keepdims=True))
        p0 = jnp.exp(s0 - m_n)
        p1 = jnp.exp(s1 - m_n)
        l_n = jnp.sum(p0, axis=1, keepdims=True) + jnp.sum(p1, axis=1, keepdims=True)
        pps.append(jnp.concatenate([p0, p1], axis=0).astype(BF16))
        m_all = jnp.where(lane128 == n, m_n, m_all)
        l_all = jnp.where(lane128 == n, l_n, l_all)
    outs = []
    for n in range(n_blocks):
        vv = jnp.concatenate([as_keys(v_refs[2 * n]), as_keys(v_refs[2 * n + 1])], axis=1)
        o2 = jnp.dot(pps[n], vv, preferred_element_type=F32)
        outs.append(o2[:nrow, :HEAD_DIM] + o2[nrow:, HEAD_DIM:])

    sel = _topk_select(g_all, lane128 < n_blocks, n_blocks, axis=1)
    pad = jnp.zeros((LANES - nrow, HEAD_DIM), F32)
    own_k = jnp.concatenate([rows_of(kn_ref), pad], axis=0).astype(BF16)
    own_v = jnp.concatenate([rows_of(vn_ref), pad], axis=0).astype(BF16)
    s_own = masked_scores(lax.dot_general(qb, own_k, _NT, preferred_element_type=F32), past_len, LANES,
                          lambda lane, kpos: (lane < nrow) & (kpos <= tq))
    m_fin = jnp.maximum(jnp.max(jnp.where(sel, m_all, NEG), axis=1, keepdims=True),
                        jnp.max(s_own, axis=1, keepdims=True))
    w = jnp.where(sel, jnp.exp(jnp.where(sel, m_all - m_fin, 0.0)), 0.0)
    p_own = jnp.exp(s_own - m_fin)
    l_fin = jnp.sum(w * l_all, axis=1, keepdims=True) + jnp.sum(p_own, axis=1, keepdims=True)
    comb = jnp.dot(p_own.astype(BF16), own_v, preferred_element_type=F32)
    for n in range(n_blocks):
        comb = comb + w[:, n:n + 1] * outs[n]
    out = (comb / l_fin).astype(o_ref.dtype)
    o_ref[seq_row, :] = jnp.concatenate([out[j:j + 1, :] for j in range(nrow)], axis=1)


def _moba_sample(q, k_new, v_new, cache_k, cache_v, layer, page_table, slopes_np, past_len):
    db, width = q.shape
    page, n_heads, hd = cache_k.shape[2:]
    nrow = width // hd
    n_q = nrow // n_heads
    n_pages = page_table.shape[1]
    n_blocks = past_len // MOBA_BLOCK
    assert past_len % MOBA_BLOCK == 0 and MOBA_BLOCK == 2 * page and n_pages * page == past_len
    assert n_blocks >= MOBA_TOPK and n_blocks <= LANES and n_heads == SUBLANES
    assert nrow % 16 == 0 and nrow <= LANES and db % SUBLANES == 0
    slope_col = np.tile(slopes_np, n_q).reshape(nrow, 1)
    seqspec = pl.BlockSpec((SUBLANES, width), lambda s, pt: (s // SUBLANES, 0))
    pagespecs = [pl.BlockSpec((None, None, page, n_heads, hd),
                              lambda s, pt, j=j: (layer, pt[s * n_pages + j], 0, 0, 0))
                 for j in range(n_pages)]
    grid_spec = pltpu.PrefetchScalarGridSpec(
        num_scalar_prefetch=1,
        grid=(db,),
        in_specs=[seqspec, seqspec, seqspec, pl.BlockSpec((nrow, 1), lambda s, pt: (0, 0))] + pagespecs * 2,
        out_specs=seqspec,
    )
    return pl.pallas_call(
        functools.partial(_moba_sample_kernel, n_heads=n_heads, n_q=n_q, past_len=past_len, n_blocks=n_blocks),
        grid_spec=grid_spec,
        out_shape=jax.ShapeDtypeStruct((db, width), F32),
        compiler_params=_cparams(("arbitrary",)),
        name="moba_sample",
    )(page_table.reshape(-1), q, k_new, v_new, jnp.asarray(slope_col),
      *([cache_k] * n_pages), *([cache_v] * n_pages))


def _outproj_kernel(oa_ref, u_ref, vg_ref, x_ref, gt_ref, sc_ref, sh_ref, ga_ref, gg_ref, gf_ref,
                    wo_ref, ws_ref, bs_ref, x1_ref, h2_ref, og_scr, *, col_slabs, n_groups):
    tm = x_ref.shape[0]
    da = n_groups * GROUP_DIM
    d = gf_ref.shape[1]
    if col_slabs:
        n_t = ws_ref.shape[0]
        for t in range(n_t):
            mixed = jnp.zeros((tm, da), F32) + bs_ref[t:t + 1, :]
            for s in range(t + 1):
                mixed = mixed + ws_ref[t, s:s + 1, :] * vg_ref[:, s * da:(s + 1) * da].astype(F32)
            og_scr[:, t * da:(t + 1) * da] = u_ref[:, t * da:(t + 1) * da] * mixed
        parts = [(slice(None), t) for t in range(n_t)]
    else:
        pm = min(tm, 256)
        parts = [(slice(r * pm, (r + 1) * pm), 0) for r in range(tm // pm)]
        tri = (lax.broadcasted_iota(jnp.int32, (CHUNK, CHUNK), 0)
               >= lax.broadcasted_iota(jnp.int32, (CHUNK, CHUNK), 1))
        for gi in range(n_groups):
            sl = slice(gi * GROUP_DIM, (gi + 1) * GROUP_DIM)
            w = jnp.where(tri, ws_ref[gi], 0.0).astype(BF16)
            for c in range(tm // CHUNK):
                rs = slice(c * CHUNK, (c + 1) * CHUNK)
                mixed = jnp.dot(w, vg_ref[rs, sl].astype(BF16), preferred_element_type=F32) + bs_ref[:, sl]
                og_scr[rs, sl] = u_ref[rs, sl] * mixed
    gt, sc, sh = gt_ref[...], sc_ref[...], sh_ref[...]
    for rows, t in parts:
        ca, cd = slice(t * da, (t + 1) * da), slice(t * d, (t + 1) * d)
        a_n = _rms(oa_ref[rows, ca], ga_ref[...]).astype(BF16)
        g_n = _rms(og_scr[rows, ca], gg_ref[...]).astype(BF16)
        out = (jnp.dot(a_n, wo_ref[0:da, :], preferred_element_type=F32)
               + jnp.dot(g_n, wo_ref[da:, :], preferred_element_type=F32))
        x1 = x_ref[rows, cd] + gt * out
        x1_ref[rows, cd] = x1
        h2 = _rms(x1, gf_ref[...]) * (1.0 + sc) + sh
        h2_ref[rows, cd] = h2.astype(h2_ref.dtype)


def _outproj(oa, u, vg, x, mod, g_oa, g_og, g_ffn, w_out_bf, ws, bs, *, tm, tiles_per_batch, col_slabs):
    rows = x.shape[0]
    d = g_ffn.shape[0]
    da = g_oa.shape[0]
    r = min(mod.shape[1], tm)
    modspec = lambda m: pl.BlockSpec((None, r, d), lambda i, m=m: (i // tiles_per_batch, 0, m))
    row = lambda a: pl.BlockSpec((tm, a.shape[1]), lambda i: (i, 0))
    full = lambda a: pl.BlockSpec(a.shape, lambda i, nd=a.ndim: (0,) * nd, pipeline_mode=pl.Buffered(1))
    g_oa, g_og, g_ffn = g_oa.reshape(1, -1), g_og.reshape(1, -1), g_ffn.reshape(1, -1)
    return pl.pallas_call(
        functools.partial(_outproj_kernel, col_slabs=col_slabs, n_groups=da // GROUP_DIM),
        grid=(rows // tm,),
        in_specs=[row(oa), row(u), row(vg), row(x), modspec(2), modspec(4), modspec(3),
                  full(g_oa), full(g_og), full(g_ffn), full(w_out_bf), full(ws), full(bs)],
        out_specs=[row(x), row(x)],
        out_shape=[jax.ShapeDtypeStruct(x.shape, F32), jax.ShapeDtypeStruct(x.shape, BF16)],
        scratch_shapes=[pltpu.VMEM((tm, oa.shape[1]), F32)],
        compiler_params=_cparams(("arbitrary",)),
        name="gmlp_out_proj",
    )(oa, u, vg, x, mod, mod, mod, g_oa, g_og, g_ffn, w_out_bf, ws, bs)


def _ffn_kernel(h_ref, x1_ref, gt_ref, wa_ref, wp_ref, wc_ref, bc_ref, wd_ref, prev_ref,
                y_ref, conv_ref, abuf, halo, *stack, shift, tiles_per_batch, n_parts, n_slabs):
    i = pl.program_id(0)
    f = pl.program_id(1)
    nf = pl.num_programs(1)
    d = wd_ref.shape[1]
    keep = (CONV_W - 1) * shift
    off = -(-keep // SUBLANES) * SUBLANES
    tm = abuf.shape[0] - off
    if n_slabs > 1:
        h_src, = stack

        @pl.when(f == 0)
        def _():
            for t in range(n_slabs):
                h_src[t * shift:(t + 1) * shift, :] = h_ref[:, t * d:(t + 1) * d]
    else:
        h_src = h_ref

    @pl.when(i % tiles_per_batch == 0)
    def _():
        if n_slabs > 1:
            for j in range(CONV_W - 1):
                abuf[off - keep + j * shift:off - keep + (j + 1) * shift, :] = prev_ref[:, j, :]
        else:
            abuf[off - keep:off, :] = prev_ref[...]

    @pl.when(i % tiles_per_batch != 0)
    def _():
        abuf[off - keep:off, :] = halo[f]

    @pl.when(f == 0)
    def _():
        y_ref[...] = jnp.zeros(y_ref.shape, F32)

    def accumulate(row0, val):
        if n_slabs == 1:
            y_ref[row0:row0 + val.shape[0], :] += val
        else:
            for k in range(val.shape[0] // shift):
                t = row0 // shift + k
                y_ref[:, t * d:(t + 1) * d] += val[k * shift:(k + 1) * shift, :]

    hm = tm // n_parts
    a_parts, p_parts = [], []
    for r in range(n_parts):
        rows = slice(r * hm, (r + 1) * hm)
        h = h_src[rows, :]
        a = jnp.dot(h, wa_ref[...], preferred_element_type=F32)
        abuf[off + r * hm:off + (r + 1) * hm, :] = a
        a_parts.append(a)
        p_parts.append(jnp.dot(h, wp_ref[...], preferred_element_type=F32))
    for r in range(n_parts):
        rows = slice(r * hm, (r + 1) * hm)
        ac = bc_ref[...] + wc_ref[CONV_W - 1:CONV_W, :] * a_parts[r]
        for jj in range(CONV_W - 1):
            start = off + r * hm - (CONV_W - 1 - jj) * shift
            ac = ac + wc_ref[jj:jj + 1, :] * abuf[start:start + hm, :]
        gate = (ac * jax.nn.sigmoid(ac)) * p_parts[r]
        accumulate(r * hm, jnp.dot(gate.astype(BF16), wd_ref[...], preferred_element_type=F32))
    last = abuf[off + tm - keep:off + tm, :]
    halo[f] = last
    if n_slabs > 1:
        for j in range(CONV_W - 1):
            conv_ref[:, j, :] = last[j * shift:(j + 1) * shift, :]
    else:
        conv_ref[...] = last

    @pl.when(f == nf - 1)
    def _():
        gt = gt_ref[...]
        for t in range(n_slabs):
            cols = slice(t * d, (t + 1) * d)
            y_ref[:, cols] = x1_ref[:, cols] + gt * y_ref[:, cols]


def _ffn(h2, x1, mod, w_up_bf, w_conv, b_conv, w_down_bf, conv_prev, *, tm, tf, tiles_per_batch, shift,
         n_slabs=1):
    d = w_down_bf.shape[1]
    ff = w_down_bf.shape[0]
    nf = ff // tf
    keep = (CONV_W - 1) * shift
    off = -(-keep // SUBLANES) * SUBLANES
    n_tiles = 1 if n_slabs > 1 else x1.shape[0] // tm
    blk_rows = tm // n_slabs
    r = min(mod.shape[1], blk_rows)
    stack = [pltpu.VMEM((tm, d), BF16)] if n_slabs > 1 else []
    if n_slabs > 1:
        hist_in = pl.BlockSpec((shift, CONV_W - 1, tf), lambda i, f: (0, 0, f))
        hist_out = hist_in
        hist_shape = (shift, CONV_W - 1, ff)
    else:
        hist_in = pl.BlockSpec((None, keep, tf), lambda i, f: (i // tiles_per_batch, 0, f))
        hist_out = pl.BlockSpec((None, keep, tf), lambda i, f: (i, 0, f))
        hist_shape = (n_tiles, keep, ff)
    y, tail = pl.pallas_call(
        functools.partial(_ffn_kernel, shift=shift, tiles_per_batch=tiles_per_batch,
                          n_parts=tm // FFN_PART_ROWS, n_slabs=n_slabs),
        grid=(n_tiles, nf),
        in_specs=[pl.BlockSpec((blk_rows, n_slabs * d), lambda i, f: (i, 0)),
                  pl.BlockSpec((blk_rows, n_slabs * d), lambda i, f: (i, 0)),
                  pl.BlockSpec((None, r, d), lambda i, f: (i // tiles_per_batch, 0, 5)),
                  pl.BlockSpec((d, tf), lambda i, f: (0, f)),
                  pl.BlockSpec((d, tf), lambda i, f: (0, f + nf)),
                  pl.BlockSpec((CONV_W, tf), lambda i, f: (0, f)),
                  pl.BlockSpec((1, tf), lambda i, f: (0, f)),
                  pl.BlockSpec((tf, d), lambda i, f: (f, 0)),
                  hist_in],
        out_specs=[pl.BlockSpec((blk_rows, n_slabs * d), lambda i, f: (i, 0)), hist_out],
        out_shape=[jax.ShapeDtypeStruct(x1.shape, F32), jax.ShapeDtypeStruct(hist_shape, F32)],
        scratch_shapes=[pltpu.VMEM((off + tm, tf), F32), pltpu.VMEM((nf, keep, tf), F32)] + stack,
        compiler_params=_cparams(("arbitrary", "arbitrary")),
        name="conv_ffn",
    )(h2, x1, mod, w_up_bf, w_up_bf, w_conv, b_conv.reshape(1, ff), w_down_bf, conv_prev)
    if n_slabs > 1:
        return y, tail
    return y, tail[tiles_per_batch - 1::tiles_per_batch]


def kernel(x_prompt, x_sample, cache_k, cache_v, state_conv, page_table, c_prompt, c_sample, w_ada, b_ada, g_norm_mix, w_in, g_q, g_k, g_v, w_s, b_s, g_out_attn, g_out_gmlp, w_out, g_norm_ffn, w_up, w_conv, b_conv, w_down):
    depth = w_ada.shape[0]
    b, s, d = x_prompt.shape
    db, n_q, _ = x_sample.shape
    page, n_heads, hd = cache_k.shape[2:]
    da = n_heads * hd
    ff = w_down.shape[1]
    n_groups = w_s.shape[1]
    past_len = page_table.shape[1] * page
    assert hd == HEAD_DIM and da == n_groups * GROUP_DIM and s % CHUNK == 0 and n_q <= CHUNK
    assert w_in.shape[2] == 5 * da
    slopes_np = (2.0 ** (-8.0 * np.arange(1, n_heads + 1) / n_heads)).astype(np.float32)
    slopes = jnp.asarray(slopes_np)

    n_c = b + db
    pad_c = -(-n_c // SUBLANES) * SUBLANES
    c_all = jnp.concatenate([c_sample, c_prompt, jnp.zeros((pad_c - n_c, d), F32)], axis=0)

    h_p = x_prompt.reshape(b * s, d)
    h_s = x_sample.reshape(db, n_q * d)
    tm_p = 512
    tm_ffn = 512
    outs = [[] for _ in range(7)]
    for layer in range(depth):
        w_in_bf = w_in[layer].astype(BF16)
        w_out_bf = w_out[layer].astype(BF16)
        w_up_bf = w_up[layer].astype(BF16)
        w_down_bf = w_down[layer].astype(BF16)

        mod = _ada(c_all, w_ada[layer], b_ada[layer])
        mod_p = mod[db:n_c].reshape(b, 1, N_MOD * d)
        mod_s = mod.reshape(1, pad_c, N_MOD * d)

        q, k, v, u, vg = _inproj(h_p, mod_p, g_norm_mix[layer], w_in_bf, g_q[layer], g_k[layer], g_v[layer],
                                 tm=tm_p, tiles_per_batch=s // tm_p, q_dtype=BF16, vg_dtype=BF16)
        o_attn = _moba_prompt(q.reshape(b, s, da), k.reshape(b, s, da), v.reshape(b, s, da), slopes)
        bs_rows = jnp.repeat(b_s[layer].T, GROUP_DIM, axis=1)
        x1, h2 = _outproj(o_attn.reshape(b * s, da), u, vg, h_p, mod_p, g_out_attn[layer], g_out_gmlp[layer],
                          g_norm_ffn[layer], w_out_bf, w_s[layer], bs_rows,
                          tm=tm_p, tiles_per_batch=s // tm_p, col_slabs=False)
        conv0 = jnp.zeros((b, CONV_W - 1, ff), F32)
        y_p, conv_p = _ffn(h2, x1, mod_p, w_up_bf, w_conv[layer], b_conv[layer], w_down_bf, conv0,
                           tm=tm_ffn, tf=512, tiles_per_batch=s // tm_ffn, shift=1)

        qs, ks, vs, us, vgs = _inproj(h_s, mod_s, g_norm_mix[layer], w_in_bf, g_q[layer], g_k[layer], g_v[layer],
                                      tm=db, tiles_per_batch=n_q, q_dtype=F32, vg_dtype=F32, col_slabs=True)
        o_s = _moba_sample(qs, ks, vs, cache_k, cache_v, layer, page_table, slopes_np, past_len)
        ws_t = jnp.repeat(jnp.tril(w_s[layer][:, :n_q, :n_q]).transpose(1, 2, 0), GROUP_DIM, axis=2)
        bs_t = jnp.repeat(b_s[layer][:, :n_q].T, GROUP_DIM, axis=1)
        x1s, h2s = _outproj(o_s, us, vgs, h_s, mod_s, g_out_attn[layer], g_out_gmlp[layer], g_norm_ffn[layer],
                            w_out_bf, ws_t, bs_t, tm=db, tiles_per_batch=1, col_slabs=True)
        y_s, conv_s = _ffn(h2s, x1s, mod_s, w_up_bf, w_conv[layer], b_conv[layer], w_down_bf, state_conv[layer],
                           tm=n_q * db, tf=512, tiles_per_batch=1, shift=db, n_slabs=n_q)

        h_p, h_s = y_p, y_s
        outs[0].append(k.reshape(b, s // page, page, n_heads, hd))
        outs[1].append(v.reshape(b, s // page, page, n_heads, hd))
        outs[2].append(ks.reshape(db, n_q, n_heads, hd))
        outs[3].append(vs.reshape(db, n_q, n_heads, hd))
        outs[4].append(conv_p)
        outs[5].append(conv_s)
        outs[6].append(vgs.reshape(db, n_q, da))

    y_prompt = h_p.reshape(b, s, d)
    y_sample = h_s.reshape(db, n_q, d)
    return (y_prompt, y_sample) + tuple(jnp.stack(o) for o in outs)
```
